```python
import jax, jax.numpy as jnp
from jax import lax
import numpy as np


D_MODEL = 1024
BATCH = 4
SEQ = 8192
DEPTH = 1

CHUNK = 64
LN_EPS = 1e-5
ALPHA = (2.0 * DEPTH) ** 0.25
BETA = (8.0 * DEPTH) ** -0.25

A_HEAD = 64
A_WIDTH = D_MODEL // 2
A_HEADS = A_WIDTH // A_HEAD
A_DECAY_LORA = 64
A_ICLR_LORA = 64
A_GATE_LORA = 128
A_GN_EPS = 64e-5
A_IN = 3 * A_WIDTH + A_DECAY_LORA + A_ICLR_LORA + A_GATE_LORA
A_SPLITS = (A_WIDTH, 2 * A_WIDTH, 3 * A_WIDTH, 3 * A_WIDTH + A_DECAY_LORA,
            3 * A_WIDTH + A_DECAY_LORA + A_ICLR_LORA)

B_WIDTH = D_MODEL // 2
B_HEADS = 4
B_DV = B_WIDTH // B_HEADS
B_DK = B_DV // 2
B_WIDTH_K = B_HEADS * B_DK
B_GATE_LORA = 16
B_GATE_NORM = 16.0
B_CONV = 4
B_RMS_EPS = 1e-5
B_QKV = 2 * B_WIDTH_K + B_WIDTH
B_IN = B_QKV + B_GATE_LORA + B_WIDTH
B_SPLITS = (B_QKV, B_QKV + B_GATE_LORA)

D_MIX = A_WIDTH + B_WIDTH
IN_COLS = A_IN + B_IN

P_HEADS = 8
P_KEYS = 128
P_EXPERTS = P_KEYS * P_KEYS
P_HALF = 128
P_TOPK = 16
P_TOKEN_BLOCK = 128

kernel_name = "hymba_rwkv7_gla_peer_deepnorm"


def layer_norm(x, g, b):
    xf = x.astype(jnp.float32)
    mu = jnp.mean(xf, axis=-1, keepdims=True)
    var = jnp.mean(jnp.square(xf - mu), axis=-1, keepdims=True)
    return ((xf - mu) * lax.rsqrt(var + LN_EPS) * g + b).astype(x.dtype)


def token_shift(p, mu):
    prev = jnp.pad(p, ((0, 0), (1, 0), (0, 0)))[:, :-1]
    return p + (prev - p) * mu


def causal_conv(p, w):
    T = p.shape[1]
    pad = jnp.pad(p, ((0, 0), (B_CONV - 1, 0), (0, 0)))
    return sum(w[j] * pad[:, j:j + T] for j in range(B_CONV))


def rwkv7_scan(r, decay, k, v, kk, b):
    Bn, _, H, N = r.shape

    def step(S, inp):
        r_t, w_t, k_t, v_t, kk_t, b_t = inp
        sa = jnp.einsum('bhvk,bhk->bhv', S, -kk_t)
        S = S * w_t[:, :, None, :] + sa[..., None] * b_t[:, :, None, :] + v_t[..., None] * k_t[:, :, None, :]
        return S, jnp.einsum('bhvk,bhk->bhv', S, r_t)

    xs = tuple(jnp.moveaxis(t, 1, 0) for t in (r, decay, k, v, kk, b))
    S0 = jnp.zeros((Bn, H, N, N), jnp.float32)
    _, y = lax.scan(step, S0, xs)
    return jnp.moveaxis(y, 0, 1)


def rwkv7_group(p, mu, w0, w2, a0, a2, g2, k_k, k_a, r_k, gn_g, gn_b):
    Bn, T, _ = p.shape
    p = token_shift(p, mu)
    r, k, v, wd, ad, gd = jnp.split(p, A_SPLITS, axis=-1)
    w = -jax.nn.softplus(-(w0 + jnp.tanh(wd) @ w2)) - 0.5
    a = jax.nn.sigmoid(a0 + ad @ a2)
    g = jax.nn.sigmoid(gd) @ g2

    def hd(t):
        return t.astype(jnp.float32).reshape(Bn, T, A_HEADS, A_HEAD)

    kk = hd(k * k_k)
    kk = kk / jnp.maximum(jnp.sqrt(jnp.sum(kk * kk, axis=-1, keepdims=True)), 1e-12)
    a_h = hd(a)
    k_h = hd(k * (1 + (a - 1) * k_a))
    r_h, v_h = hd(r), hd(v)
    decay = jnp.exp(-jnp.exp(hd(w)))
    y = rwkv7_scan(r_h, decay, k_h, v_h, kk, kk * a_h)
    mean = jnp.mean(y, axis=-1, keepdims=True)
    var = jnp.mean(jnp.square(y - mean), axis=-1, keepdims=True)
    y = ((y - mean) * lax.rsqrt(var + A_GN_EPS)).reshape(Bn, T, A_WIDTH) * gn_g + gn_b
    bonus = jnp.sum(r_h * k_h * r_k, axis=-1, keepdims=True) * v_h
    y = y + bonus.reshape(Bn, T, A_WIDTH)
    return (y * g).astype(p.dtype)


def gla_chunked(q, k, v, gk):
    Bn, T, H, DK = q.shape
    n = T // CHUNK

    def to_chunks(t):
        return jnp.moveaxis(t.reshape(Bn, n, CHUNK, H, t.shape[-1]), 1, 0)

    causal = jnp.tril(jnp.ones((CHUNK, CHUNK), bool))[None, :, :, None, None]

    def step(S, inp):
        qc, kc, vc, gc = inp
        G = jnp.cumsum(gc, axis=1)
        o_inter = jnp.einsum('bihd,bhdv->bihv', qc * jnp.exp(G), S)
        diff = G[:, :, None] - G[:, None]
        dec = jnp.exp(jnp.where(causal, diff, -jnp.inf))
        A = jnp.einsum('bihd,bjhd,bijhd->bhij', qc, kc, dec)
        o_intra = jnp.einsum('bhij,bjhv->bihv', A, vc)
        G_last = G[:, -1]
        k_dec = kc * jnp.exp(G_last[:, None] - G)
        S = S * jnp.exp(G_last)[..., None] + jnp.einsum('bjhd,bjhv->bhdv', k_dec, vc)
        return S, o_inter + o_intra

    S0 = jnp.zeros((Bn, H, DK, v.shape[-1]), jnp.float32)
    _, o = lax.scan(step, S0, (to_chunks(q), to_chunks(k), to_chunks(v), to_chunks(gk)))
    return jnp.moveaxis(o, 0, 1).reshape(Bn, T, H, v.shape[-1])


def gla_group(p, conv_w, gk_w2, gk_b, norm_g):
    Bn, T, _ = p.shape
    qkv, gkd, g = jnp.split(p, B_SPLITS, axis=-1)
    qkv = jax.nn.silu(causal_conv(qkv, conv_w))
    q, k, v = jnp.split(qkv, (B_WIDTH_K, 2 * B_WIDTH_K), axis=-1)
    gk = jax.nn.log_sigmoid((gkd @ gk_w2 + gk_b).astype(jnp.float32)) / B_GATE_NORM

    def hd(t, d):
        return t.astype(jnp.float32).reshape(Bn, T, B_HEADS, d)

    o = gla_chunked(hd(q, B_DK) * (B_DK ** -0.5), hd(k, B_DK), hd(v, B_DV), hd(gk, B_DK))
    o = o * lax.rsqrt(jnp.mean(o * o, axis=-1, keepdims=True) + B_RMS_EPS) * norm_g
    o = o * jax.nn.silu(hd(g, B_DV))
    return o.reshape(Bn, T, B_WIDTH).astype(p.dtype)


def peer(h, w_q, sub_keys, u_tab, v_tab):
    Bn, T, D = h.shape
    n_tok = Bn * T
    tok = h.reshape(n_tok, D)
    q = (tok @ w_q).reshape(n_tok, P_HEADS, 2, P_HALF).astype(jnp.float32)
    s = jnp.einsum('nhpc,hpkc->nhpk', q, sub_keys.astype(jnp.float32))
    sv, si = lax.top_k(s, P_TOPK)
    cand = sv[:, :, 0, :, None] + sv[:, :, 1, None, :]
    cv, ci = lax.top_k(cand.reshape(n_tok, P_HEADS, P_TOPK * P_TOPK), P_TOPK)
    e1 = jnp.take_along_axis(si[:, :, 0], ci // P_TOPK, axis=-1)
    e2 = jnp.take_along_axis(si[:, :, 1], ci % P_TOPK, axis=-1)
    ids = (e1 * P_KEYS + e2).reshape(n_tok // P_TOKEN_BLOCK, P_TOKEN_BLOCK, P_HEADS * P_TOPK)
    gates = jax.nn.softmax(cv, axis=-1).reshape(n_tok // P_TOKEN_BLOCK, P_TOKEN_BLOCK, P_HEADS * P_TOPK)
    xb = tok.reshape(n_tok // P_TOKEN_BLOCK, P_TOKEN_BLOCK, D)

    def block(args):
        x_b, id_b, g_b = args
        u = u_tab[id_b]
        act = jax.nn.gelu(jnp.einsum('td,tkd->tk', x_b, u), approximate=False) * g_b
        return jnp.einsum('tk,tkd->td', act, v_tab[id_b]).astype(h.dtype)

    out = lax.map(block, (xb, ids, gates))
    return out.reshape(Bn, T, D)


def setup_inputs(seed: int = 0) -> dict:
    key = jax.random.key(seed)
    ks = iter(jax.random.split(key, 40))
    L = DEPTH

    def nrm(shape, scale):
        return jax.random.normal(next(ks), shape, jnp.float32) * scale

    def gain(shape):
        return 1.0 + nrm(shape, 0.02)

    def uni(shape, lo, hi):
        return jax.random.uniform(next(ks), shape, jnp.float32, lo, hi)

    return {
        'x': nrm((BATCH, SEQ, D_MODEL), 1.0),
        'ln_in_g': gain((D_MODEL,)),
        'ln_in_b': nrm((D_MODEL,), 0.02),
        'w_in': nrm((L, D_MODEL, IN_COLS), D_MODEL ** -0.5),
        'a_mu': uni((L, A_IN), 0.0, 1.0),
        'a_w0': uni((L, A_WIDTH), -5.0, -0.5),
        'a_w2': nrm((L, A_DECAY_LORA, A_WIDTH), 0.5 * A_DECAY_LORA ** -0.5),
        'a_a0': nrm((L, A_WIDTH), 0.1),
        'a_a2': nrm((L, A_ICLR_LORA, A_WIDTH), A_ICLR_LORA ** -0.5),
        'a_g2': nrm((L, A_GATE_LORA, A_WIDTH), A_GATE_LORA ** -0.5),
        'a_k_k': 0.85 + nrm((L, A_WIDTH), 0.02),
        'a_k_a': gain((L, A_WIDTH)),
        'a_r_k': nrm((L, A_HEADS, A_HEAD), 0.1),
        'a_gn_g': gain((L, A_WIDTH)),
        'a_gn_b': nrm((L, A_WIDTH), 0.02),
        'b_conv': nrm((L, B_CONV, B_QKV), B_CONV ** -0.5),
        'b_gk_w2': nrm((L, B_GATE_LORA, B_WIDTH_K), B_GATE_LORA ** -0.5),
        'b_gk_b': nrm((L, B_WIDTH_K), 0.1),
        'b_norm_g': gain((L, B_DV)),
        'w_out': nrm((L, D_MIX, D_MODEL), BETA * D_MIX ** -0.5),
        'ln1_g': gain((L, D_MODEL)),
        'ln1_b': nrm((L, D_MODEL), 0.02),
        'p_wq': nrm((L, D_MODEL, P_HEADS * 2 * P_HALF), D_MODEL ** -0.5),
        'p_keys': nrm((L, P_HEADS, 2, P_KEYS, P_HALF), P_HALF ** -0.5),
        'p_u': nrm((L, P_EXPERTS, D_MODEL), D_MODEL ** -0.5),
        'p_v': nrm((L, P_EXPERTS, D_MODEL), BETA),
        'ln2_g': gain((L, D_MODEL)),
        'ln2_b': nrm((L, D_MODEL), 0.02),
    }


def reference(x, ln_in_g, ln_in_b, w_in, a_mu, a_w0, a_w2, a_a0, a_a2, a_g2, a_k_k, a_k_a,
              a_r_k, a_gn_g, a_gn_b, b_conv, b_gk_w2, b_gk_b, b_norm_g, w_out, ln1_g, ln1_b,
              p_wq, p_keys, p_u, p_v, ln2_g, ln2_b):
    h = layer_norm(x, ln_in_g, ln_in_b)
    for l in range(DEPTH):
        p = h @ w_in[l]
        pa, pb = jnp.split(p, (A_IN,), axis=-1)
        ya = rwkv7_group(pa, a_mu[l], a_w0[l], a_w2[l], a_a0[l], a_a2[l], a_g2[l],
                         a_k_k[l], a_k_a[l], a_r_k[l], a_gn_g[l], a_gn_b[l])
        yb = gla_group(pb, b_conv[l], b_gk_w2[l], b_gk_b[l], b_norm_g[l])
        y = jnp.concatenate([ya, yb], axis=-1) @ w_out[l]
        h = layer_norm(ALPHA * h + y, ln1_g[l], ln1_b[l])
        f = peer(h, p_wq[l], p_keys[l], p_u[l], p_v[l])
        h = layer_norm(ALPHA * h + f, ln2_g[l], ln2_b[l])
    return h
```

```python
import functools
import math

import jax
import jax.numpy as jnp
from jax import lax
from jax.experimental import pallas as pl
from jax.experimental.pallas import tpu as pltpu

F32 = jnp.float32
BF16 = jnp.bfloat16
I32 = jnp.int32

D_MODEL = 1024
DEPTH = 1
CHUNK = 64
LN_EPS = 1e-5
ALPHA = (2.0 * DEPTH) ** 0.25

A_HEAD = 64
A_WIDTH = 512
A_HEADS = 8
A_GN_EPS = 64e-5
A_IN = 1792
A_PAD = 1920

B_HEADS = 4
B_DV = 128
B_DK = 64
B_WIDTH = 512
B_WIDTH_K = 256
B_QKV = 1024
B_GATE_LORA = 16
B_GATE_NORM = 16.0
B_CONV = 4
B_RMS_EPS = 1e-5
B_PAD = 1664

P_HEADS = 8
P_KEYS = 128
P_HALF = 128
P_TOPK = 16
P_EXPERTS = P_KEYS * P_KEYS

LANES = 128
G_PITCH = 136

VMEM_LIMIT = 56 * 1024 * 1024


def _dot(a, b):
    return jnp.dot(a.astype(BF16), b.astype(BF16), preferred_element_type=F32)


def _dot_nt(a, b):
    return lax.dot_general(a.astype(BF16), b.astype(BF16), (((1,), (1,)), ((), ())),
                           preferred_element_type=F32)


def _dot_tn(a, b):
    return lax.dot_general(a.astype(BF16), b.astype(BF16), (((0,), (0,)), ((), ())),
                           preferred_element_type=F32)


def _split3(x):
    hi = x.astype(BF16)
    r1 = x - hi.astype(F32)
    mid = r1.astype(BF16)
    lo = (r1 - mid.astype(F32)).astype(BF16)
    return hi, mid, lo


def _dot_exact_rhs(a01, x):
    a = a01.astype(BF16)
    hi, mid, lo = _split3(x)
    out = jnp.dot(a, hi, preferred_element_type=F32)
    out += jnp.dot(a, mid, preferred_element_type=F32)
    out += jnp.dot(a, lo, preferred_element_type=F32)
    return out


def _dot_exact_lhs(x, b01):
    b = b01.astype(BF16)
    hi, mid, lo = _split3(x)
    out = jnp.dot(hi, b, preferred_element_type=F32)
    out += jnp.dot(mid, b, preferred_element_type=F32)
    out += jnp.dot(lo, b, preferred_element_type=F32)
    return out


def _sigmoid(x):
    return 1.0 / (1.0 + jnp.exp(-x))


def _softplus(x):
    return jnp.maximum(x, 0.0) + jnp.log(1.0 + jnp.exp(-jnp.abs(x)))


def _silu(x):
    return x * _sigmoid(x)


def _layer_norm(x, g, b):
    mu = jnp.mean(x, axis=-1, keepdims=True)
    xc = x - mu
    var = jnp.mean(xc * xc, axis=-1, keepdims=True)
    return xc * lax.rsqrt(var + LN_EPS) * g + b


def _block_diag_ones(n, seg):
    r = lax.broadcasted_iota(I32, (n, n), 0) // seg
    c = lax.broadcasted_iota(I32, (n, n), 1) // seg
    return jnp.where(r == c, 1.0, 0.0).astype(F32)


def _chunk_tril(n, seg):
    r = lax.broadcasted_iota(I32, (n, n), 0)
    c = lax.broadcasted_iota(I32, (n, n), 1)
    return jnp.where((r // seg == c // seg) & (c <= r), 1.0, 0.0).astype(F32)


def _ln_proj_kernel(x_ref, g_ref, b_ref, wa_ref, wb_ref, h_ref, pa_ref, pb_ref):
    h = _layer_norm(x_ref[...], g_ref[...], b_ref[...])
    h_ref[...] = h
    hb = h.astype(BF16)
    pa_ref[...] = jnp.dot(hb, wa_ref[...], preferred_element_type=F32)
    pb_ref[...] = jnp.dot(hb, wb_ref[...], preferred_element_type=F32)


def ln_proj(x2, g, b, wa, wb, tm=512):
    n = x2.shape[0]
    full = lambda i: (0, 0)
    return pl.pallas_call(
        _ln_proj_kernel,
        grid=(n // tm,),
        in_specs=[
            pl.BlockSpec((tm, D_MODEL), lambda i: (i, 0)),
            pl.BlockSpec((1, D_MODEL), full),
            pl.BlockSpec((1, D_MODEL), full),
            pl.BlockSpec((D_MODEL, A_PAD), full),
            pl.BlockSpec((D_MODEL, B_PAD), full),
        ],
        out_specs=[
            pl.BlockSpec((tm, D_MODEL), lambda i: (i, 0)),
            pl.BlockSpec((tm, A_PAD), lambda i: (i, 0)),
            pl.BlockSpec((tm, B_PAD), lambda i: (i, 0)),
        ],
        out_shape=[
            jax.ShapeDtypeStruct((n, D_MODEL), F32),
            jax.ShapeDtypeStruct((n, A_PAD), F32),
            jax.ShapeDtypeStruct((n, B_PAD), F32),
        ],
        compiler_params=pltpu.CompilerParams(
            dimension_semantics=("arbitrary",), vmem_limit_bytes=VMEM_LIMIT),
        name="ln_proj",
    )(x2, g, b, wa, wb)


def _rwkv_kernel(pa_ref, mu_ref, w0_ref, w2_ref, a0_ref, a2_ref, g2_ref, kkw_ref, kaw_ref,
                 rk_ref, gng_ref, gnb_ref, ya_ref,
                 carry_ref, z_ref, at_s, rt_s, bt_s, kt_s, bh_s, kh_s, v_s, el_s, y_s, *, tb):
    nck = tb // CHUNK
    w = A_WIDTH

    @pl.when(pl.program_id(1) == 0)
    def _():
        carry_ref[...] = jnp.zeros_like(carry_ref)
        z_ref[...] = jnp.zeros_like(z_ref)

    p = pa_ref[0]
    rolled = pltpu.roll(p, 1, axis=0)
    row = lax.broadcasted_iota(I32, p.shape, 0)
    prev = jnp.where(row == 0, carry_ref[0:1, :], rolled)
    carry_ref[0:1, :] = p[tb - 1:tb, :]
    ps = p + (prev - p) * mu_ref[...]

    r = ps[:, 0:w]
    k = ps[:, w:2 * w]
    v = ps[:, 2 * w:3 * w]
    wd = ps[:, 3 * w:3 * w + LANES]
    ad = ps[:, 3 * w + LANES:3 * w + 2 * LANES]
    gd = ps[:, 3 * w + 2 * LANES:3 * w + 3 * LANES]

    wlin = w0_ref[...] + _dot(jnp.tanh(wd), w2_ref[...])
    lw = -jnp.exp(-_softplus(-wlin) - 0.5)
    iclr = _sigmoid(a0_ref[...] + _dot(ad, a2_ref[...]))
    gate = _dot(_sigmoid(gd), g2_ref[...])

    bd = _block_diag_ones(w, A_HEAD)
    kkr = k * kkw_ref[...]
    ss = _dot_exact_lhs(kkr * kkr, bd)
    kk = kkr / jnp.maximum(jnp.sqrt(ss), 1e-12)
    kh = k * (1.0 + (iclr - 1.0) * kaw_ref[...])
    b = kk * iclr
    bonus = _dot_exact_lhs(r * kh * rk_ref[...], bd) * v

    lcum = _dot_exact_rhs(_chunk_tril(tb, CHUNK), lw)
    e_l = jnp.exp(lcum)
    e_nl = jnp.exp(-lcum)
    at_s[...] = -kk * jnp.exp(lcum - lw)
    rt_s[...] = r * e_l
    bt_s[...] = b * e_nl
    kt_s[...] = kh * e_nl
    v_s[...] = v
    for c in range(nck):
        rows = slice(c * CHUNK, (c + 1) * CHUNK)
        l_end = lcum[(c + 1) * CHUNK - 1:(c + 1) * CHUNK, :]
        e_lc = jnp.exp(l_end - lcum[rows, :])
        bh_s[rows, :] = b[rows, :] * e_lc
        kh_s[rows, :] = kh[rows, :] * e_lc
        el_s[c:c + 1, :] = jnp.exp(l_end)

    ri = lax.broadcasted_iota(I32, (2 * CHUNK, 2 * CHUNK), 0)
    ci = lax.broadcasted_iota(I32, (2 * CHUNK, 2 * CHUNK), 1) % CHUNK
    tri_mask = ci < jnp.where(ri < CHUNK, ri, ri - (CHUNK - 1))
    er = lax.broadcasted_iota(I32, (CHUNK, CHUNK), 0)
    ec = lax.broadcasted_iota(I32, (CHUNK, CHUNK), 1)
    eye_mask = er == ec
    eye = jnp.where(eye_mask, 1.0, 0.0).astype(F32)

    def chunk_body(c, carry):
        r0 = pl.multiple_of(c * CHUNK, CHUNK)
        at_c = at_s[pl.ds(r0, CHUNK), :]
        rt_c = rt_s[pl.ds(r0, CHUNK), :]
        bt_c = bt_s[pl.ds(r0, CHUNK), :]
        kt_c = kt_s[pl.ds(r0, CHUNK), :]
        bh_c = bh_s[pl.ds(r0, CHUNK), :]
        kh_c = kh_s[pl.ds(r0, CHUNK), :]
        v_c = v_s[pl.ds(r0, CHUNK), :]
        el_c = el_s[pl.ds(c, 1), :]
        ys = []
        for h in range(A_HEADS):
            sl = slice(h * A_HEAD, (h + 1) * A_HEAD)
            x1 = jnp.concatenate([at_c[:, sl], rt_c[:, sl]], axis=0)
            x2 = jnp.concatenate([bt_c[:, sl], kt_c[:, sl]], axis=0)
            gm = jnp.where(tri_mask, _dot_nt(x1, x2), 0.0)
            mab = gm[0:CHUNK, 0:CHUNK]
            tm = eye + mab
            pw = mab
            for _ in range(5):
                pw = _dot(pw, pw)
                tm = tm + _dot(tm, pw)
            vh = v_c[:, sl]
            mv = _dot(gm[:, CHUNK:2 * CHUNK], vh)
            p12 = _dot(tm, jnp.concatenate([mv[0:CHUNK], at_c[:, sl]], axis=1))
            yq = jnp.concatenate([mv[CHUNK:2 * CHUNK], rt_c[:, sl]], axis=1) + _dot(
                gm[CHUNK:2 * CHUNK, 0:CHUNK], p12)
            dg = jnp.where(eye_mask, jnp.broadcast_to(el_c[:, sl], (CHUNK, CHUNK)), 0.0)
            zw = jnp.concatenate([_dot_tn(kh_c[:, sl], vh), dg], axis=1) + _dot_tn(bh_c[:, sl], p12)
            z = z_ref[h]
            ys.append(yq[:, 0:CHUNK] + _dot(yq[:, CHUNK:2 * CHUNK], z))
            z_ref[h] = zw[:, 0:CHUNK] + _dot(zw[:, CHUNK:2 * CHUNK], z)
        y_s[pl.ds(r0, CHUNK), :] = jnp.concatenate(ys, axis=1)
        return carry

    lax.fori_loop(0, nck, chunk_body, 0)

    y = y_s[...]
    inv_n = 1.0 / A_HEAD
    mean = _dot_exact_lhs(y, bd) * inv_n
    yc = y - mean
    var = _dot_exact_lhs(yc * yc, bd) * inv_n
    yn = yc * lax.rsqrt(var + A_GN_EPS) * gng_ref[...] + gnb_ref[...]
    ya_ref[0] = (yn + bonus) * gate


def rwkv7(pa3, mu, w0, w2, a0, a2, g2, kkw, kaw, rk, gng, gnb, tb=256):
    bsz, t, _ = pa3.shape
    full = lambda bi, ti: (0, 0)
    vec = pl.BlockSpec((1, A_WIDTH), full)
    slab = pltpu.VMEM((tb, A_WIDTH), F32)
    return pl.pallas_call(
        functools.partial(_rwkv_kernel, tb=tb),
        grid=(bsz, t // tb),
        in_specs=[
            pl.BlockSpec((1, tb, A_PAD), lambda bi, ti: (bi, ti, 0)),
            pl.BlockSpec((1, A_PAD), full),
            vec,
            pl.BlockSpec((LANES, A_WIDTH), full),
            vec,
            pl.BlockSpec((LANES, A_WIDTH), full),
            pl.BlockSpec((LANES, A_WIDTH), full),
            vec, vec, vec, vec, vec,
        ],
        out_specs=pl.BlockSpec((1, tb, A_WIDTH), lambda bi, ti: (bi, ti, 0)),
        out_shape=jax.ShapeDtypeStruct((bsz, t, A_WIDTH), F32),
        scratch_shapes=[
            pltpu.VMEM((8, A_PAD), F32),
            pltpu.VMEM((A_HEADS, A_HEAD, A_HEAD), F32),
            slab, slab, slab, slab, slab, slab, slab,
            pltpu.VMEM((8, A_WIDTH), F32),
            slab,
        ],
        compiler_params=pltpu.CompilerParams(
            dimension_semantics=("arbitrary", "arbitrary"), vmem_limit_bytes=VMEM_LIMIT),
        name="rwkv7",
    )(pa3, mu, w0, w2, a0, a2, g2, kkw, kaw, rk, gng, gnb)


def _gla_kernel(pb_ref, cw_ref, gw_ref, gb_ref, ng_ref, yb_ref,
                xpad_ref, s_ref, qt_s, kt_s, kh_s, v_s, gk_s, o_s, *, tb):
    nck = tb // CHUNK

    @pl.when(pl.program_id(1) == 0)
    def _():
        xpad_ref[0:8, :] = jnp.zeros((8, B_QKV), F32)
        s_ref[...] = jnp.zeros_like(s_ref)

    p = pb_ref[0]
    xpad_ref[8:8 + tb, :] = p[:, 0:B_QKV]
    conv = cw_ref[B_CONV - 1:B_CONV, :] * p[:, 0:B_QKV]
    for j in range(B_CONV - 1):
        sh = B_CONV - 1 - j
        conv = conv + cw_ref[j:j + 1, :] * xpad_ref[8 - sh:8 - sh + tb, :]
    xpad_ref[0:8, :] = p[tb - 8:tb, 0:B_QKV]
    qkv = _silu(conv)
    q = qkv[:, 0:B_WIDTH_K] * (B_DK ** -0.5)
    k = qkv[:, B_WIDTH_K:2 * B_WIDTH_K]
    v_s[...] = qkv[:, 2 * B_WIDTH_K:]
    gkd = p[:, B_QKV:B_QKV + LANES]
    gz = _dot(gkd, gw_ref[...]) + gb_ref[...]
    gk = -_softplus(-gz) * (1.0 / B_GATE_NORM)
    gk_s[...] = gk
    gcum = _dot_exact_rhs(_chunk_tril(tb, CHUNK), gk)
    qt_s[...] = q * jnp.exp(gcum)
    kt_s[...] = k * jnp.exp(-gcum)
    for c in range(nck):
        rows = slice(c * CHUNK, (c + 1) * CHUNK)
        g_end = gcum[(c + 1) * CHUNK - 1:(c + 1) * CHUNK, :]
        kh_s[rows, :] = k[rows, :] * jnp.exp(g_end - gcum[rows, :])

    ri = lax.broadcasted_iota(I32, (CHUNK, CHUNK), 0)
    ci = lax.broadcasted_iota(I32, (CHUNK, CHUNK), 1)
    causal = ci <= ri
    ones_cv = jnp.ones((CHUNK, B_DV), F32)

    def chunk_body(c, carry):
        r0 = pl.multiple_of(c * CHUNK, CHUNK)
        qt_c = qt_s[pl.ds(r0, CHUNK), :]
        kt_c = kt_s[pl.ds(r0, CHUNK), :]
        kh_c = kh_s[pl.ds(r0, CHUNK), :]
        v_c = v_s[pl.ds(r0, CHUNK), :]
        gk_c = gk_s[pl.ds(r0, CHUNK), :]
        os_ = []
        for h in range(B_HEADS):
            sk = slice(h * B_DK, (h + 1) * B_DK)
            sv = slice(h * B_DV, (h + 1) * B_DV)
            a = jnp.where(causal, _dot_nt(qt_c[:, sk], kt_c[:, sk]), 0.0)
            s = s_ref[h]
            os_.append(_dot(a, v_c[:, sv]) + _dot(qt_c[:, sk], s))
            hi, mid, lo = _split3(gk_c[:, sk])
            ob = ones_cv.astype(BF16)
            dn = (((0,), (0,)), ((), ()))
            gcol = (lax.dot_general(hi, ob, dn, preferred_element_type=F32)
                    + lax.dot_general(mid, ob, dn, preferred_element_type=F32)
                    + lax.dot_general(lo, ob, dn, preferred_element_type=F32))
            s_ref[h] = s * jnp.exp(gcol) + _dot_tn(kh_c[:, sk], v_c[:, sv])
        o_s[pl.ds(r0, CHUNK), :] = jnp.concatenate(os_, axis=1)
        return carry

    lax.fori_loop(0, nck, chunk_body, 0)

    g = p[:, B_QKV + LANES:]
    outs = []
    for h in range(B_HEADS):
        sv = slice(h * B_DV, (h + 1) * B_DV)
        o = o_s[:, sv]
        o = o * lax.rsqrt(jnp.mean(o * o, axis=-1, keepdims=True) + B_RMS_EPS) * ng_ref[...]
        outs.append(o * _silu(g[:, sv]))
    yb_ref[0] = jnp.concatenate(outs, axis=1)


def gla(pb3, cw, gw, gb, ng, tb=256):
    bsz, t, _ = pb3.shape
    full = lambda bi, ti: (0, 0)
    return pl.pallas_call(
        functools.partial(_gla_kernel, tb=tb),
        grid=(bsz, t // tb),
        in_specs=[
            pl.BlockSpec((1, tb, B_PAD), lambda bi, ti: (bi, ti, 0)),
            pl.BlockSpec((B_CONV, B_QKV), full),
            pl.BlockSpec((LANES, B_WIDTH_K), full),
            pl.BlockSpec((1, B_WIDTH_K), full),
            pl.BlockSpec((1, B_DV), full),
        ],
        out_specs=pl.BlockSpec((1, tb, B_WIDTH), lambda bi, ti: (bi, ti, 0)),
        out_shape=jax.ShapeDtypeStruct((bsz, t, B_WIDTH), F32),
        scratch_shapes=[
            pltpu.VMEM((tb + 8, B_QKV), F32),
            pltpu.VMEM((B_HEADS, B_DK, B_DV), F32),
            pltpu.VMEM((tb, B_WIDTH_K), F32),
            pltpu.VMEM((tb, B_WIDTH_K), F32),
            pltpu.VMEM((tb, B_WIDTH_K), F32),
            pltpu.VMEM((tb, B_WIDTH), F32),
            pltpu.VMEM((tb, B_WIDTH_K), F32),
            pltpu.VMEM((tb, B_WIDTH), F32),
        ],
        compiler_params=pltpu.CompilerParams(
            dimension_semantics=("arbitrary", "arbitrary"), vmem_limit_bytes=VMEM_LIMIT),
        name="gla",
    )(pb3, cw, gw, gb, ng)


def _out_ln_kernel(ya_ref, yb_ref, h_ref, wa_ref, wb_ref, g_ref, b_ref, h1_ref, h1b_ref):
    y = _dot(ya_ref[...], wa_ref[...]) + _dot(yb_ref[...], wb_ref[...])
    h1 = _layer_norm(ALPHA * h_ref[...] + y, g_ref[...], b_ref[...])
    h1_ref[...] = h1
    h1b_ref[...] = h1.astype(BF16)


def out_ln(ya, yb, h, wa, wb, g, b, tm=512):
    n = h.shape[0]
    full = lambda i: (0, 0)
    return pl.pallas_call(
        _out_ln_kernel,
        grid=(n // tm,),
        in_specs=[
            pl.BlockSpec((tm, A_WIDTH), lambda i: (i, 0)),
            pl.BlockSpec((tm, B_WIDTH), lambda i: (i, 0)),
            pl.BlockSpec((tm, D_MODEL), lambda i: (i, 0)),
            pl.BlockSpec((A_WIDTH, D_MODEL), full),
            pl.BlockSpec((B_WIDTH, D_MODEL), full),
            pl.BlockSpec((1, D_MODEL), full),
            pl.BlockSpec((1, D_MODEL), full),
        ],
        out_specs=[
            pl.BlockSpec((tm, D_MODEL), lambda i: (i, 0)),
            pl.BlockSpec((tm, D_MODEL), lambda i: (i, 0)),
        ],
        out_shape=[
            jax.ShapeDtypeStruct((n, D_MODEL), F32),
            jax.ShapeDtypeStruct((n, D_MODEL), BF16),
        ],
        compiler_params=pltpu.CompilerParams(
            dimension_semantics=("arbitrary",), vmem_limit_bytes=VMEM_LIMIT),
        name="out_ln",
    )(ya, yb, h, wa, wb, g, b)


def _fold_keys_kernel(keys_ref, wq_ref, out_ref):
    kh, km, kl = _split3(keys_ref[0])
    wh, wm, wl = _split3(wq_ref[...])
    dn = (((1,), (1,)), ((), ()))
    acc = lax.dot_general(kh, wh, dn, preferred_element_type=F32)
    for a, b in ((kh, wm), (km, wh), (km, wm), (kh, wl), (kl, wh)):
        acc += lax.dot_general(a, b, dn, preferred_element_type=F32)
    out_ref[...] = acc.astype(BF16)


def fold_keys(keys3, wq):
    nhp = keys3.shape[0]
    return pl.pallas_call(
        _fold_keys_kernel,
        grid=(nhp,),
        in_specs=[
            pl.BlockSpec((1, P_KEYS, P_HALF), lambda i: (i, 0, 0)),
            pl.BlockSpec((D_MODEL, P_HALF), lambda i: (0, i)),
        ],
        out_specs=pl.BlockSpec((P_KEYS, D_MODEL), lambda i: (i, 0)),
        out_shape=jax.ShapeDtypeStruct((nhp * P_KEYS, D_MODEL), BF16),
        compiler_params=pltpu.CompilerParams(dimension_semantics=("arbitrary",)),
        name="fold_keys",
    )(keys3, wq)


def _top16(vals):
    nrow = vals.shape[0]
    idx = lax.broadcasted_iota(I32, vals.shape, 0).astype(F32)
    out_v, out_i = [], []
    for _ in range(P_TOPK):
        m = jnp.max(vals, axis=0, keepdims=True)
        i = jnp.min(jnp.where(vals == m, idx, float(nrow)), axis=0, keepdims=True)
        out_v.append(m)
        out_i.append(i)
        vals = jnp.where(idx == i, -jnp.inf, vals)
    return jnp.concatenate(out_v, axis=0), jnp.concatenate(out_i, axis=0)


def _pick16(table, sel):
    out = jnp.zeros_like(table)
    for i in range(P_TOPK):
        out = jnp.where(sel == float(i), table[i:i + 1, :], out)
    return out


def _peer_topk_kernel(ws_ref, h_ref, e1_ref, e2_ref, gt_ref, s_ref):
    s_ref[...] = lax.dot_general(ws_ref[...], h_ref[...], (((1,), (1,)), ((), ())),
                                 preferred_element_type=F32)

    def head_body(h, carry):
        r0 = pl.multiple_of(h * 2 * P_KEYS, 2 * P_KEYS)
        sv1, si1 = _top16(s_ref[pl.ds(r0, P_KEYS), :])
        sv2, si2 = _top16(s_ref[pl.ds(r0 + P_KEYS, P_KEYS), :])
        cand = jnp.concatenate([sv1[i:i + 1, :] + sv2 for i in range(P_TOPK)], axis=0)
        cv, cidx = _top16(cand)
        qi = jnp.floor(cidx * (1.0 / P_TOPK))
        ri = cidx - qi * P_TOPK
        e1 = _pick16(si1, qi)
        e2 = _pick16(si2, ri)
        ex = jnp.exp(cv - cv[0:1, :])
        gates = ex / jnp.sum(ex, axis=0, keepdims=True)
        o0 = pl.multiple_of(h * P_TOPK, P_TOPK)
        e1_ref[pl.ds(o0, P_TOPK), :] = e1.astype(I32)
        e2_ref[pl.ds(o0, P_TOPK), :] = e2.astype(I32)
        gt_ref[pl.ds(o0, P_TOPK), :] = gates
        return carry

    lax.fori_loop(0, P_HEADS, head_body, 0)


def peer_topk(ws, h1b, tt=256):
    n = h1b.shape[0]
    nsel = P_HEADS * P_TOPK
    nrow = P_HEADS * 2 * P_KEYS
    out = jax.ShapeDtypeStruct((nsel, n), I32)
    spec = pl.BlockSpec((nsel, tt), lambda i: (0, i))
    return pl.pallas_call(
        _peer_topk_kernel,
        grid=(n // tt,),
        in_specs=[
            pl.BlockSpec((nrow, D_MODEL), lambda i: (0, 0)),
            pl.BlockSpec((tt, D_MODEL), lambda i: (i, 0)),
        ],
        out_specs=[spec, spec, spec],
        out_shape=[out, out, jax.ShapeDtypeStruct((nsel, n), F32)],
        scratch_shapes=[pltpu.VMEM((nrow, tt), F32)],
        compiler_params=pltpu.CompilerParams(
            dimension_semantics=("arbitrary",), vmem_limit_bytes=VMEM_LIMIT),
        name="peer_topk",
    )(ws, h1b)


def _gelu(x):
    return 0.5 * x * (1.0 + lax.erf(x * (1.0 / math.sqrt(2.0))))


def _peer_ffn_kernel(h1_ref, h1b_ref, e1_ref, e2_ref, gt_ref, u_ref, v_ref, g_ref, b_ref,
                     out_ref, gs_ref, acc_ref, *, tt, ec):
    j = pl.program_id(1)
    ng = ec // P_KEYS

    @pl.when(j == 0)
    def _():
        sub = lax.broadcasted_iota(I32, (P_KEYS, P_HEADS * P_TOPK), 0)

        def tok_body(n, carry):
            e1 = e1_ref[pl.ds(n, 1), :]
            e2 = e2_ref[pl.ds(n, 1), :]
            gt = gt_ref[pl.ds(n, 1), :]
            a_t = jnp.where(sub == e1, gt, 0.0).astype(BF16)
            b_t = jnp.where(sub == e2, 1.0, 0.0).astype(BF16)
            gn = lax.dot_general(a_t, b_t, (((1,), (1,)), ((), ())), preferred_element_type=F32)
            gs_ref[pl.ds(pl.multiple_of(n * G_PITCH, 8), P_KEYS), :] = gn
            return carry

        lax.fori_loop(0, tt, tok_body, 0)
        acc_ref[...] = jnp.zeros_like(acc_ref)

    gj = jnp.concatenate(
        [gs_ref[pl.ds(j * ng + i, tt, stride=G_PITCH), :] for i in range(ng)], axis=1)
    s = lax.dot_general(h1b_ref[...], u_ref[...], (((1,), (1,)), ((), ())),
                        preferred_element_type=F32)
    hmat = (gj * _gelu(s)).astype(BF16)
    acc_ref[...] += jnp.dot(hmat, v_ref[...], preferred_element_type=F32)

    @pl.when(j == pl.num_programs(1) - 1)
    def _():
        out_ref[...] = _layer_norm(ALPHA * h1_ref[...] + acc_ref[...], g_ref[...], b_ref[...])


def peer_ffn(h1, h1b, e1t, e2t, gtt, ub, vb, g, b, tt=512, ec=256):
    n = h1.shape[0]
    nsel = P_HEADS * P_TOPK
    tok = lambda i, j: (i, 0)
    full = lambda i, j: (0, 0)
    return pl.pallas_call(
        functools.partial(_peer_ffn_kernel, tt=tt, ec=ec),
        grid=(n // tt, P_EXPERTS // ec),
        in_specs=[
            pl.BlockSpec((tt, D_MODEL), tok),
            pl.BlockSpec((tt, D_MODEL), tok),
            pl.BlockSpec((tt, nsel), tok),
            pl.BlockSpec((tt, nsel), tok),
            pl.BlockSpec((tt, nsel), tok),
            pl.BlockSpec((ec, D_MODEL), lambda i, j: (j, 0)),
            pl.BlockSpec((ec, D_MODEL), lambda i, j: (j, 0)),
            pl.BlockSpec((1, D_MODEL), full),
            pl.BlockSpec((1, D_MODEL), full),
        ],
        out_specs=pl.BlockSpec((tt, D_MODEL), tok),
        out_shape=jax.ShapeDtypeStruct((n, D_MODEL), F32),
        scratch_shapes=[
            pltpu.VMEM((tt * G_PITCH, P_KEYS), F32),
            pltpu.VMEM((tt, D_MODEL), F32),
        ],
        compiler_params=pltpu.CompilerParams(
            dimension_semantics=("arbitrary", "arbitrary"), vmem_limit_bytes=VMEM_LIMIT),
        name="peer_ffn",
    )(h1, h1b, e1t, e2t, gtt, ub, vb, g, b)


def _pad_cols(w, width):
    return jnp.pad(w, ((0, 0), (0, width - w.shape[1])))


def _pad_rows(w, height):
    return jnp.pad(w, ((0, height - w.shape[0]), (0, 0)))


def _layout_a(w):
    w3 = 3 * A_WIDTH
    return jnp.concatenate([
        w[:, 0:w3], _pad_cols(w[:, w3:w3 + 64], LANES), _pad_cols(w[:, w3 + 64:w3 + 128], LANES),
        w[:, w3 + 128:A_IN]], axis=1)


def _layout_b(w):
    return jnp.concatenate([
        w[:, 0:B_QKV], _pad_cols(w[:, B_QKV:B_QKV + B_GATE_LORA], LANES),
        w[:, B_QKV + B_GATE_LORA:]], axis=1)


def kernel(x, ln_in_g, ln_in_b, w_in, a_mu, a_w0, a_w2, a_a0, a_a2, a_g2, a_k_k, a_k_a, a_r_k,
           a_gn_g, a_gn_b, b_conv, b_gk_w2, b_gk_b, b_norm_g, w_out, ln1_g, ln1_b, p_wq, p_keys,
           p_u, p_v, ln2_g, ln2_b):
    bsz, t, d = x.shape
    n = bsz * t
    row = lambda a: a.reshape(1, -1)

    h = x.reshape(n, d)
    for l in range(DEPTH):
        wa = _layout_a(w_in[l][:, :A_IN]).astype(BF16)
        wb = _layout_b(w_in[l][:, A_IN:]).astype(BF16)
        if l == 0:
            h, pa, pb = ln_proj(h, row(ln_in_g), row(ln_in_b), wa, wb)
        ya = rwkv7(
            pa.reshape(bsz, t, A_PAD), _layout_a(row(a_mu[l])), row(a_w0[l]),
            _pad_rows(a_w2[l], LANES), row(a_a0[l]), _pad_rows(a_a2[l], LANES), a_g2[l],
            row(a_k_k[l]), row(a_k_a[l]), row(a_r_k[l]), row(a_gn_g[l]), row(a_gn_b[l]))
        yb = gla(pb.reshape(bsz, t, B_PAD), b_conv[l], _pad_rows(b_gk_w2[l], LANES),
                 row(b_gk_b[l]), row(b_norm_g[l]))
        h1, h1b = out_ln(ya.reshape(n, A_WIDTH), yb.reshape(n, B_WIDTH), h,
                         w_out[l][:A_WIDTH].astype(BF16), w_out[l][A_WIDTH:].astype(BF16),
                         row(ln1_g[l]), row(ln1_b[l]))
        ws = fold_keys(p_keys[l].reshape(P_HEADS * 2, P_KEYS, P_HALF), p_wq[l])
        e1, e2, gt = peer_topk(ws, h1b)
        h = peer_ffn(h1, h1b, e1.T, e2.T, gt.T, p_u[l].astype(BF16), p_v[l].astype(BF16),
                     row(ln2_g[l]), row(ln2_b[l]))
    return h.reshape(bsz, t, d)
```

```python
import functools
import math

import jax
import jax.numpy as jnp
from jax import lax
from jax.experimental import pallas as pl
from jax.experimental.pallas import tpu as pltpu

F32 = jnp.float32
BF16 = jnp.bfloat16
I32 = jnp.int32

D_MODEL = 1024
DEPTH = 1
CHUNK = 64
LN_EPS = 1e-5
ALPHA = (2.0 * DEPTH) ** 0.25

A_HEAD = 64
A_WIDTH = 512
A_HEADS = 8
A_GN_EPS = 64e-5
A_IN = 1792
A_PAD = 1920

B_HEADS = 4
B_DV = 128
B_DK = 64
B_WIDTH = 512
B_WIDTH_K = 256
B_QKV = 1024
B_GATE_LORA = 16
B_GATE_NORM = 16.0
B_CONV = 4
B_RMS_EPS = 1e-5
B_PAD = 1664

P_HEADS = 8
P_KEYS = 128
P_HALF = 128
P_TOPK = 16
P_EXPERTS = P_KEYS * P_KEYS

LANES = 128
G_PITCH = 136

VMEM_LIMIT = 56 * 1024 * 1024


def _dot(a, b):
    return jnp.dot(a.astype(BF16), b.astype(BF16), preferred_element_type=F32)


def _dot_nt(a, b):
    return lax.dot_general(a.astype(BF16), b.astype(BF16), (((1,), (1,)), ((), ())),
                           preferred_element_type=F32)


def _dot_tn(a, b):
    return lax.dot_general(a.astype(BF16), b.astype(BF16), (((0,), (0,)), ((), ())),
                           preferred_element_type=F32)


def _split3(x):
    hi = x.astype(BF16)
    r1 = x - hi.astype(F32)
    mid = r1.astype(BF16)
    lo = (r1 - mid.astype(F32)).astype(BF16)
    return hi, mid, lo


def _dot_exact_rhs(a01, x):
    a = a01.astype(BF16)
    hi, mid, lo = _split3(x)
    out = jnp.dot(a, hi, preferred_element_type=F32)
    out += jnp.dot(a, mid, preferred_element_type=F32)
    out += jnp.dot(a, lo, preferred_element_type=F32)
    return out


def _dot_exact_lhs(x, b01):
    b = b01.astype(BF16)
    hi, mid, lo = _split3(x)
    out = jnp.dot(hi, b, preferred_element_type=F32)
    out += jnp.dot(mid, b, preferred_element_type=F32)
    out += jnp.dot(lo, b, preferred_element_type=F32)
    return out


def _sigmoid(x):
    return 1.0 / (1.0 + jnp.exp(-x))


def _softplus(x):
    return jnp.maximum(x, 0.0) + jnp.log(1.0 + jnp.exp(-jnp.abs(x)))


def _silu(x):
    return x * _sigmoid(x)


def _layer_norm(x, g, b):
    mu = jnp.mean(x, axis=-1, keepdims=True)
    xc = x - mu
    var = jnp.mean(xc * xc, axis=-1, keepdims=True)
    return xc * lax.rsqrt(var + LN_EPS) * g + b


def _block_diag_ones(n, seg):
    r = lax.broadcasted_iota(I32, (n, n), 0) // seg
    c = lax.broadcasted_iota(I32, (n, n), 1) // seg
    return jnp.where(r == c, 1.0, 0.0).astype(F32)


def _chunk_tril(n, seg):
    r = lax.broadcasted_iota(I32, (n, n), 0)
    c = lax.broadcasted_iota(I32, (n, n), 1)
    return jnp.where((r // seg == c // seg) & (c <= r), 1.0, 0.0).astype(F32)


def _ln_proj_kernel(x_ref, g_ref, b_ref, wa_ref, wb_ref, h_ref, pa_ref, pb_ref):
    h = _layer_norm(x_ref[...], g_ref[...], b_ref[...])
    h_ref[...] = h
    hb = h.astype(BF16)
    pa_ref[...] = jnp.dot(hb, wa_ref[...], preferred_element_type=F32)
    pb_ref[...] = jnp.dot(hb, wb_ref[...], preferred_element_type=F32)


def ln_proj(x2, g, b, wa, wb, tm=512):
    n = x2.shape[0]
    full = lambda i: (0, 0)
    return pl.pallas_call(
        _ln_proj_kernel,
        grid=(n // tm,),
        in_specs=[
            pl.BlockSpec((tm, D_MODEL), lambda i: (i, 0)),
            pl.BlockSpec((1, D_MODEL), full),
            pl.BlockSpec((1, D_MODEL), full),
            pl.BlockSpec((D_MODEL, A_PAD), full),
            pl.BlockSpec((D_MODEL, B_PAD), full),
        ],
        out_specs=[
            pl.BlockSpec((tm, D_MODEL), lambda i: (i, 0)),
            pl.BlockSpec((tm, A_PAD), lambda i: (i, 0)),
            pl.BlockSpec((tm, B_PAD), lambda i: (i, 0)),
        ],
        out_shape=[
            jax.ShapeDtypeStruct((n, D_MODEL), F32),
            jax.ShapeDtypeStruct((n, A_PAD), F32),
            jax.ShapeDtypeStruct((n, B_PAD), F32),
        ],
        compiler_params=pltpu.CompilerParams(
            dimension_semantics=("arbitrary",), vmem_limit_bytes=VMEM_LIMIT),
        name="ln_proj",
    )(x2, g, b, wa, wb)


def _rwkv_kernel(pa_ref, mu_ref, w0_ref, w2_ref, a0_ref, a2_ref, g2_ref, kkw_ref, kaw_ref,
                 rk_ref, gng_ref, gnb_ref, ya_ref,
                 carry_ref, z_ref, at_s, rt_s, bt_s, kt_s, bh_s, kh_s, v_s, el_s, y_s, *, tb):
    nck = tb // CHUNK
    w = A_WIDTH

    @pl.when(pl.program_id(1) == 0)
    def _():
        carry_ref[...] = jnp.zeros_like(carry_ref)
        z_ref[...] = jnp.zeros_like(z_ref)

    p = pa_ref[0]
    rolled = pltpu.roll(p, 1, axis=0)
    row = lax.broadcasted_iota(I32, p.shape, 0)
    prev = jnp.where(row == 0, carry_ref[0:1, :], rolled)
    carry_ref[0:1, :] = p[tb - 1:tb, :]
    ps = p + (prev - p) * mu_ref[...]

    r = ps[:, 0:w]
    k = ps[:, w:2 * w]
    v = ps[:, 2 * w:3 * w]
    wd = ps[:, 3 * w:3 * w + LANES]
    ad = ps[:, 3 * w + LANES:3 * w + 2 * LANES]
    gd = ps[:, 3 * w + 2 * LANES:3 * w + 3 * LANES]

    wlin = w0_ref[...] + _dot(jnp.tanh(wd), w2_ref[...])
    lw = -jnp.exp(-_softplus(-wlin) - 0.5)
    iclr = _sigmoid(a0_ref[...] + _dot(ad, a2_ref[...]))
    gate = _dot(_sigmoid(gd), g2_ref[...])

    bd = _block_diag_ones(w, A_HEAD)
    kkr = k * kkw_ref[...]
    ss = _dot_exact_lhs(kkr * kkr, bd)
    kk = kkr / jnp.maximum(jnp.sqrt(ss), 1e-12)
    kh = k * (1.0 + (iclr - 1.0) * kaw_ref[...])
    b = kk * iclr
    bonus = _dot_exact_lhs(r * kh * rk_ref[...], bd) * v

    lcum = _dot_exact_rhs(_chunk_tril(tb, CHUNK), lw)
    e_l = jnp.exp(lcum)
    e_nl = jnp.exp(-lcum)
    at_s[...] = -kk * jnp.exp(lcum - lw)
    rt_s[...] = r * e_l
    bt_s[...] = b * e_nl
    kt_s[...] = kh * e_nl
    v_s[...] = v
    for c in range(nck):
        rows = slice(c * CHUNK, (c + 1) * CHUNK)
        l_end = lcum[(c + 1) * CHUNK - 1:(c + 1) * CHUNK, :]
        e_lc = jnp.exp(l_end - lcum[rows, :])
        bh_s[rows, :] = b[rows, :] * e_lc
        kh_s[rows, :] = kh[rows, :] * e_lc
        el_s[c:c + 1, :] = jnp.exp(l_end)

    ri = lax.broadcasted_iota(I32, (2 * CHUNK, 2 * CHUNK), 0)
    ci = lax.broadcasted_iota(I32, (2 * CHUNK, 2 * CHUNK), 1) % CHUNK
    tri_mask = ci < jnp.where(ri < CHUNK, ri, ri - (CHUNK - 1))
    er = lax.broadcasted_iota(I32, (CHUNK, CHUNK), 0)
    ec = lax.broadcasted_iota(I32, (CHUNK, CHUNK), 1)
    eye_mask = er == ec
    eye = jnp.where(eye_mask, 1.0, 0.0).astype(F32)

    def chunk_body(c, carry):
        r0 = pl.multiple_of(c * CHUNK, CHUNK)
        at_c = at_s[pl.ds(r0, CHUNK), :]
        rt_c = rt_s[pl.ds(r0, CHUNK), :]
        bt_c = bt_s[pl.ds(r0, CHUNK), :]
        kt_c = kt_s[pl.ds(r0, CHUNK), :]
        bh_c = bh_s[pl.ds(r0, CHUNK), :]
        kh_c = kh_s[pl.ds(r0, CHUNK), :]
        v_c = v_s[pl.ds(r0, CHUNK), :]
        el_c = el_s[pl.ds(c, 1), :]
        heads = range(A_HEADS)
        sls = [slice(h * A_HEAD, (h + 1) * A_HEAD) for h in heads]
        zs = [z_ref[h] for h in heads]
        gms = [jnp.where(tri_mask,
                         _dot_nt(jnp.concatenate([at_c[:, sl], rt_c[:, sl]], axis=0),
                                 jnp.concatenate([bt_c[:, sl], kt_c[:, sl]], axis=0)), 0.0)
               for sl in sls]
        mvs = [_dot(gm[:, CHUNK:2 * CHUNK], v_c[:, sl]) for gm, sl in zip(gms, sls)]
        khv = [_dot_tn(kh_c[:, sl], v_c[:, sl]) for sl in sls]
        pws = [gm[0:CHUNK, 0:CHUNK] for gm in gms]
        tms = [eye + pw for pw in pws]
        for _ in range(5):
            pws = [_dot(pw, pw) for pw in pws]
            tms = [tm + _dot(tm, pw) for tm, pw in zip(tms, pws)]
        p12s = [_dot(tm, jnp.concatenate([mv[0:CHUNK], at_c[:, sl]], axis=1))
                for tm, mv, sl in zip(tms, mvs, sls)]
        yqs = [jnp.concatenate([mv[CHUNK:2 * CHUNK], rt_c[:, sl]], axis=1)
               + _dot(gm[CHUNK:2 * CHUNK, 0:CHUNK], p12)
               for mv, gm, p12, sl in zip(mvs, gms, p12s, sls)]
        zws = [jnp.concatenate(
                   [kv, jnp.where(eye_mask, jnp.broadcast_to(el_c[:, sl], (CHUNK, CHUNK)), 0.0)], axis=1)
               + _dot_tn(bh_c[:, sl], p12)
               for kv, p12, sl in zip(khv, p12s, sls)]
        ys = [yq[:, 0:CHUNK] + _dot(yq[:, CHUNK:2 * CHUNK], z) for yq, z in zip(yqs, zs)]
        zn = [zw[:, 0:CHUNK] + _dot(zw[:, CHUNK:2 * CHUNK], z) for zw, z in zip(zws, zs)]
        for h in heads:
            z_ref[h] = zn[h]
        y_s[pl.ds(r0, CHUNK), :] = jnp.concatenate(ys, axis=1)
        return carry

    lax.fori_loop(0, nck, chunk_body, 0)

    y = y_s[...]
    inv_n = 1.0 / A_HEAD
    mean = _dot_exact_lhs(y, bd) * inv_n
    yc = y - mean
    var = _dot_exact_lhs(yc * yc, bd) * inv_n
    yn = yc * lax.rsqrt(var + A_GN_EPS) * gng_ref[...] + gnb_ref[...]
    ya_ref[0] = (yn + bonus) * gate


def rwkv7(pa3, mu, w0, w2, a0, a2, g2, kkw, kaw, rk, gng, gnb, tb=256):
    bsz, t, _ = pa3.shape
    full = lambda bi, ti: (0, 0)
    vec = pl.BlockSpec((1, A_WIDTH), full)
    slab = pltpu.VMEM((tb, A_WIDTH), F32)
    return pl.pallas_call(
        functools.partial(_rwkv_kernel, tb=tb),
        grid=(bsz, t // tb),
        in_specs=[
            pl.BlockSpec((1, tb, A_PAD), lambda bi, ti: (bi, ti, 0)),
            pl.BlockSpec((1, A_PAD), full),
            vec,
            pl.BlockSpec((LANES, A_WIDTH), full),
            vec,
            pl.BlockSpec((LANES, A_WIDTH), full),
            pl.BlockSpec((LANES, A_WIDTH), full),
            vec, vec, vec, vec, vec,
        ],
        out_specs=pl.BlockSpec((1, tb, A_WIDTH), lambda bi, ti: (bi, ti, 0)),
        out_shape=jax.ShapeDtypeStruct((bsz, t, A_WIDTH), F32),
        scratch_shapes=[
            pltpu.VMEM((8, A_PAD), F32),
            pltpu.VMEM((A_HEADS, A_HEAD, A_HEAD), F32),
            slab, slab, slab, slab, slab, slab, slab,
            pltpu.VMEM((8, A_WIDTH), F32),
            slab,
        ],
        compiler_params=pltpu.CompilerParams(
            dimension_semantics=("arbitrary", "arbitrary"), vmem_limit_bytes=VMEM_LIMIT),
        name="rwkv7",
    )(pa3, mu, w0, w2, a0, a2, g2, kkw, kaw, rk, gng, gnb)


def _gla_kernel(pb_ref, cw_ref, gw_ref, gb_ref, ng_ref, yb_ref,
                xpad_ref, s_ref, qt_s, kt_s, kh_s, v_s, gk_s, o_s, *, tb):
    nck = tb // CHUNK

    @pl.when(pl.program_id(1) == 0)
    def _():
        xpad_ref[0:8, :] = jnp.zeros((8, B_QKV), F32)
        s_ref[...] = jnp.zeros_like(s_ref)

    p = pb_ref[0]
    xpad_ref[8:8 + tb, :] = p[:, 0:B_QKV]
    conv = cw_ref[B_CONV - 1:B_CONV, :] * p[:, 0:B_QKV]
    for j in range(B_CONV - 1):
        sh = B_CONV - 1 - j
        conv = conv + cw_ref[j:j + 1, :] * xpad_ref[8 - sh:8 - sh + tb, :]
    xpad_ref[0:8, :] = p[tb - 8:tb, 0:B_QKV]
    qkv = _silu(conv)
    q = qkv[:, 0:B_WIDTH_K] * (B_DK ** -0.5)
    k = qkv[:, B_WIDTH_K:2 * B_WIDTH_K]
    v_s[...] = qkv[:, 2 * B_WIDTH_K:]
    gkd = p[:, B_QKV:B_QKV + LANES]
    gz = _dot(gkd, gw_ref[...]) + gb_ref[...]
    gk = -_softplus(-gz) * (1.0 / B_GATE_NORM)
    gk_s[...] = gk
    gcum = _dot_exact_rhs(_chunk_tril(tb, CHUNK), gk)
    qt_s[...] = q * jnp.exp(gcum)
    kt_s[...] = k * jnp.exp(-gcum)
    for c in range(nck):
        rows = slice(c * CHUNK, (c + 1) * CHUNK)
        g_end = gcum[(c + 1) * CHUNK - 1:(c + 1) * CHUNK, :]
        kh_s[rows, :] = k[rows, :] * jnp.exp(g_end - gcum[rows, :])

    ri = lax.broadcasted_iota(I32, (CHUNK, CHUNK), 0)
    ci = lax.broadcasted_iota(I32, (CHUNK, CHUNK), 1)
    causal = ci <= ri
    ones_cv = jnp.ones((CHUNK, B_DV), F32)

    def chunk_body(c, carry):
        r0 = pl.multiple_of(c * CHUNK, CHUNK)
        qt_c = qt_s[pl.ds(r0, CHUNK), :]
        kt_c = kt_s[pl.ds(r0, CHUNK), :]
        kh_c = kh_s[pl.ds(r0, CHUNK), :]
        v_c = v_s[pl.ds(r0, CHUNK), :]
        gk_c = gk_s[pl.ds(r0, CHUNK), :]
        os_ = []
        for h in range(B_HEADS):
            sk = slice(h * B_DK, (h + 1) * B_DK)
            sv = slice(h * B_DV, (h + 1) * B_DV)
            a = jnp.where(causal, _dot_nt(qt_c[:, sk], kt_c[:, sk]), 0.0)
            s = s_ref[h]
            os_.append(_dot(a, v_c[:, sv]) + _dot(qt_c[:, sk], s))
            hi, mid, lo = _split3(gk_c[:, sk])
            ob = ones_cv.astype(BF16)
            dn = (((0,), (0,)), ((), ()))
            gcol = (lax.dot_general(hi, ob, dn, preferred_element_type=F32)
                    + lax.dot_general(mid, ob, dn, preferred_element_type=F32)
                    + lax.dot_general(lo, ob, dn, preferred_element_type=F32))
            s_ref[h] = s * jnp.exp(gcol) + _dot_tn(kh_c[:, sk], v_c[:, sv])
        o_s[pl.ds(r0, CHUNK), :] = jnp.concatenate(os_, axis=1)
        return carry

    lax.fori_loop(0, nck, chunk_body, 0)

    g = p[:, B_QKV + LANES:]
    outs = []
    for h in range(B_HEADS):
        sv = slice(h * B_DV, (h + 1) * B_DV)
        o = o_s[:, sv]
        o = o * lax.rsqrt(jnp.mean(o * o, axis=-1, keepdims=True) + B_RMS_EPS) * ng_ref[...]
        outs.append(o * _silu(g[:, sv]))
    yb_ref[0] = jnp.concatenate(outs, axis=1)


def gla(pb3, cw, gw, gb, ng, tb=256):
    bsz, t, _ = pb3.shape
    full = lambda bi, ti: (0, 0)
    return pl.pallas_call(
        functools.partial(_gla_kernel, tb=tb),
        grid=(bsz, t // tb),
        in_specs=[
            pl.BlockSpec((1, tb, B_PAD), lambda bi, ti: (bi, ti, 0)),
            pl.BlockSpec((B_CONV, B_QKV), full),
            pl.BlockSpec((LANES, B_WIDTH_K), full),
            pl.BlockSpec((1, B_WIDTH_K), full),
            pl.BlockSpec((1, B_DV), full),
        ],
        out_specs=pl.BlockSpec((1, tb, B_WIDTH), lambda bi, ti: (bi, ti, 0)),
        out_shape=jax.ShapeDtypeStruct((bsz, t, B_WIDTH), F32),
        scratch_shapes=[
            pltpu.VMEM((tb + 8, B_QKV), F32),
            pltpu.VMEM((B_HEADS, B_DK, B_DV), F32),
            pltpu.VMEM((tb, B_WIDTH_K), F32),
            pltpu.VMEM((tb, B_WIDTH_K), F32),
            pltpu.VMEM((tb, B_WIDTH_K), F32),
            pltpu.VMEM((tb, B_WIDTH), F32),
            pltpu.VMEM((tb, B_WIDTH_K), F32),
            pltpu.VMEM((tb, B_WIDTH), F32),
        ],
        compiler_params=pltpu.CompilerParams(
            dimension_semantics=("arbitrary", "arbitrary"), vmem_limit_bytes=VMEM_LIMIT),
        name="gla",
    )(pb3, cw, gw, gb, ng)


def _out_ln_kernel(ya_ref, yb_ref, h_ref, wa_ref, wb_ref, g_ref, b_ref, h1_ref, h1b_ref):
    y = _dot(ya_ref[...], wa_ref[...]) + _dot(yb_ref[...], wb_ref[...])
    h1 = _layer_norm(ALPHA * h_ref[...] + y, g_ref[...], b_ref[...])
    h1_ref[...] = h1
    h1b_ref[...] = h1.astype(BF16)


def out_ln(ya, yb, h, wa, wb, g, b, tm=512):
    n = h.shape[0]
    full = lambda i: (0, 0)
    return pl.pallas_call(
        _out_ln_kernel,
        grid=(n // tm,),
        in_specs=[
            pl.BlockSpec((tm, A_WIDTH), lambda i: (i, 0)),
            pl.BlockSpec((tm, B_WIDTH), lambda i: (i, 0)),
            pl.BlockSpec((tm, D_MODEL), lambda i: (i, 0)),
            pl.BlockSpec((A_WIDTH, D_MODEL), full),
            pl.BlockSpec((B_WIDTH, D_MODEL), full),
            pl.BlockSpec((1, D_MODEL), full),
            pl.BlockSpec((1, D_MODEL), full),
        ],
        out_specs=[
            pl.BlockSpec((tm, D_MODEL), lambda i: (i, 0)),
            pl.BlockSpec((tm, D_MODEL), lambda i: (i, 0)),
        ],
        out_shape=[
            jax.ShapeDtypeStruct((n, D_MODEL), F32),
            jax.ShapeDtypeStruct((n, D_MODEL), BF16),
        ],
        compiler_params=pltpu.CompilerParams(
            dimension_semantics=("arbitrary",), vmem_limit_bytes=VMEM_LIMIT),
        name="out_ln",
    )(ya, yb, h, wa, wb, g, b)


def _fold_keys_kernel(keys_ref, wq_ref, out_ref):
    kh, km, kl = _split3(keys_ref[0])
    wh, wm, wl = _split3(wq_ref[...])
    dn = (((1,), (1,)), ((), ()))
    acc = lax.dot_general(kh, wh, dn, preferred_element_type=F32)
    for a, b in ((kh, wm), (km, wh), (km, wm), (kh, wl), (kl, wh)):
        acc += lax.dot_general(a, b, dn, preferred_element_type=F32)
    out_ref[...] = acc.astype(BF16)


def fold_keys(keys3, wq):
    nhp = keys3.shape[0]
    return pl.pallas_call(
        _fold_keys_kernel,
        grid=(nhp,),
        in_specs=[
            pl.BlockSpec((1, P_KEYS, P_HALF), lambda i: (i, 0, 0)),
            pl.BlockSpec((D_MODEL, P_HALF), lambda i: (0, i)),
        ],
        out_specs=pl.BlockSpec((P_KEYS, D_MODEL), lambda i: (i, 0)),
        out_shape=jax.ShapeDtypeStruct((nhp * P_KEYS, D_MODEL), BF16),
        compiler_params=pltpu.CompilerParams(dimension_semantics=("arbitrary",)),
        name="fold_keys",
    )(keys3, wq)


def _top16(vals):
    nrow = vals.shape[0]
    idx = lax.broadcasted_iota(I32, vals.shape, 0).astype(F32)
    out_v, out_i = [], []
    for _ in range(P_TOPK):
        m = jnp.max(vals, axis=0, keepdims=True)
        i = jnp.min(jnp.where(vals == m, idx, float(nrow)), axis=0, keepdims=True)
        out_v.append(m)
        out_i.append(i)
        vals = jnp.where(idx == i, -jnp.inf, vals)
    return jnp.concatenate(out_v, axis=0), jnp.concatenate(out_i, axis=0)


def _pick16(table, sel):
    out = jnp.zeros_like(table)
    for i in range(P_TOPK):
        out = jnp.where(sel == float(i), table[i:i + 1, :], out)
    return out


def _peer_topk_kernel(ws_ref, h_ref, e1_ref, e2_ref, gt_ref, s_ref):
    s_ref[...] = lax.dot_general(ws_ref[...], h_ref[...], (((1,), (1,)), ((), ())),
                                 preferred_element_type=F32)

    def head_body(h, carry):
        r0 = pl.multiple_of(h * 2 * P_KEYS, 2 * P_KEYS)
        sv1, si1 = _top16(s_ref[pl.ds(r0, P_KEYS), :])
        sv2, si2 = _top16(s_ref[pl.ds(r0 + P_KEYS, P_KEYS), :])
        cand = jnp.concatenate([sv1[i:i + 1, :] + sv2 for i in range(P_TOPK)], axis=0)
        cv, cidx = _top16(cand)
        qi = jnp.floor(cidx * (1.0 / P_TOPK))
        ri = cidx - qi * P_TOPK
        e1 = _pick16(si1, qi)
        e2 = _pick16(si2, ri)
        ex = jnp.exp(cv - cv[0:1, :])
        gates = ex / jnp.sum(ex, axis=0, keepdims=True)
        o0 = pl.multiple_of(h * P_TOPK, P_TOPK)
        e1_ref[pl.ds(o0, P_TOPK), :] = e1.astype(I32)
        e2_ref[pl.ds(o0, P_TOPK), :] = e2.astype(I32)
        gt_ref[pl.ds(o0, P_TOPK), :] = gates
        return carry

    lax.fori_loop(0, P_HEADS, head_body, 0)


def peer_topk(ws, h1b, tt=256):
    n = h1b.shape[0]
    nsel = P_HEADS * P_TOPK
    nrow = P_HEADS * 2 * P_KEYS
    out = jax.ShapeDtypeStruct((nsel, n), I32)
    spec = pl.BlockSpec((nsel, tt), lambda i: (0, i))
    return pl.pallas_call(
        _peer_topk_kernel,
        grid=(n // tt,),
        in_specs=[
            pl.BlockSpec((nrow, D_MODEL), lambda i: (0, 0)),
            pl.BlockSpec((tt, D_MODEL), lambda i: (i, 0)),
        ],
        out_specs=[spec, spec, spec],
        out_shape=[out, out, jax.ShapeDtypeStruct((nsel, n), F32)],
        scratch_shapes=[pltpu.VMEM((nrow, tt), F32)],
        compiler_params=pltpu.CompilerParams(
            dimension_semantics=("arbitrary",), vmem_limit_bytes=VMEM_LIMIT),
        name="peer_topk",
    )(ws, h1b)


def _gelu(x):
    return 0.5 * x * (1.0 + lax.erf(x * (1.0 / math.sqrt(2.0))))


def _peer_ffn_kernel(h1_ref, h1b_ref, e1_ref, e2_ref, gt_ref, u_ref, v_ref, g_ref, b_ref,
                     out_ref, gs_ref, acc_ref, *, tt, ec):
    j = pl.program_id(1)
    ng = ec // P_KEYS

    @pl.when(j == 0)
    def _():
        acc_ref[...] = jnp.zeros_like(acc_ref)
        sub = lax.broadcasted_iota(I32, (P_KEYS, P_HEADS * P_TOPK), 0)

        def tok_body(n, carry):
            e1 = e1_ref[pl.ds(n, 1), :]
            e2 = e2_ref[pl.ds(n, 1), :]
            gt = gt_ref[pl.ds(n, 1), :]
            a_t = jnp.where(sub == e1, gt, 0.0).astype(BF16)
            b_t = jnp.where(sub == e2, 1.0, 0.0).astype(BF16)
            gn = lax.dot_general(a_t, b_t, (((1,), (1,)), ((), ())), preferred_element_type=F32)
            gs_ref[pl.ds(pl.multiple_of(n * G_PITCH, 8), P_KEYS), :] = gn
            return carry

        lax.fori_loop(0, tt, tok_body, 0, unroll=8)

    x = h1b_ref[...]
    sub = 2 * P_KEYS
    hs = []
    for i in range(ec // sub):
        gi = jnp.concatenate(
            [gs_ref[pl.ds(j * ng + 2 * i + q, tt, stride=G_PITCH), :] for q in range(2)], axis=1)
        si = lax.dot_general(x, u_ref[i * sub:(i + 1) * sub, :], (((1,), (1,)), ((), ())),
                             preferred_element_type=F32)
        hs.append((gi * _gelu(si)).astype(BF16))
    hmat = jnp.concatenate(hs, axis=1)
    acc_ref[...] += jnp.dot(hmat, v_ref[...], preferred_element_type=F32)

    @pl.when(j == pl.num_programs(1) - 1)
    def _():
        out_ref[...] = _layer_norm(ALPHA * h1_ref[...] + acc_ref[...], g_ref[...], b_ref[...])


def peer_ffn(h1, h1b, e1t, e2t, gtt, ub, vb, g, b, tt=512, ec=512):
    n = h1.shape[0]
    nsel = P_HEADS * P_TOPK
    tok = lambda i, j: (i, 0)
    full = lambda i, j: (0, 0)
    once = pl.Buffered(1)
    return pl.pallas_call(
        functools.partial(_peer_ffn_kernel, tt=tt, ec=ec),
        grid=(n // tt, P_EXPERTS // ec),
        in_specs=[
            pl.BlockSpec((tt, D_MODEL), tok, pipeline_mode=once),
            pl.BlockSpec((tt, D_MODEL), tok, pipeline_mode=once),
            pl.BlockSpec((tt, nsel), tok, pipeline_mode=once),
            pl.BlockSpec((tt, nsel), tok, pipeline_mode=once),
            pl.BlockSpec((tt, nsel), tok, pipeline_mode=once),
            pl.BlockSpec((ec, D_MODEL), lambda i, j: (j, 0)),
            pl.BlockSpec((ec, D_MODEL), lambda i, j: (j, 0)),
            pl.BlockSpec((1, D_MODEL), full),
            pl.BlockSpec((1, D_MODEL), full),
        ],
        out_specs=pl.BlockSpec((tt, D_MODEL), tok),
        out_shape=jax.ShapeDtypeStruct((n, D_MODEL), F32),
        scratch_shapes=[
            pltpu.VMEM((tt * G_PITCH, P_KEYS), F32),
            pltpu.VMEM((tt, D_MODEL), F32),
        ],
        compiler_params=pltpu.CompilerParams(
            dimension_semantics=("arbitrary", "arbitrary"), vmem_limit_bytes=VMEM_LIMIT),
        name="peer_ffn",
    )(h1, h1b, e1t, e2t, gtt, ub, vb, g, b)


def _pad_cols(w, width):
    return jnp.pad(w, ((0, 0), (0, width - w.shape[1])))


def _pad_rows(w, height):
    return jnp.pad(w, ((0, height - w.shape[0]), (0, 0)))


def _layout_a(w):
    w3 = 3 * A_WIDTH
    return jnp.concatenate([
        w[:, 0:w3], _pad_cols(w[:, w3:w3 + 64], LANES), _pad_cols(w[:, w3 + 64:w3 + 128], LANES),
        w[:, w3 + 128:A_IN]], axis=1)


def _layout_b(w):
    return jnp.concatenate([
        w[:, 0:B_QKV], _pad_cols(w[:, B_QKV:B_QKV + B_GATE_LORA], LANES),
        w[:, B_QKV + B_GATE_LORA:]], axis=1)


def kernel(x, ln_in_g, ln_in_b, w_in, a_mu, a_w0, a_w2, a_a0, a_a2, a_g2, a_k_k, a_k_a, a_r_k,
           a_gn_g, a_gn_b, b_conv, b_gk_w2, b_gk_b, b_norm_g, w_out, ln1_g, ln1_b, p_wq, p_keys,
           p_u, p_v, ln2_g, ln2_b):
    bsz, t, d = x.shape
    n = bsz * t
    row = lambda a: a.reshape(1, -1)

    h = x.reshape(n, d)
    for l in range(DEPTH):
        wa = _layout_a(w_in[l][:, :A_IN]).astype(BF16)
        wb = _layout_b(w_in[l][:, A_IN:]).astype(BF16)
        if l == 0:
            h, pa, pb = ln_proj(h, row(ln_in_g), row(ln_in_b), wa, wb)
        ya = rwkv7(
            pa.reshape(bsz, t, A_PAD), _layout_a(row(a_mu[l])), row(a_w0[l]),
            _pad_rows(a_w2[l], LANES), row(a_a0[l]), _pad_rows(a_a2[l], LANES), a_g2[l],
            row(a_k_k[l]), row(a_k_a[l]), row(a_r_k[l]), row(a_gn_g[l]), row(a_gn_b[l]))
        yb = gla(pb.reshape(bsz, t, B_PAD), b_conv[l], _pad_rows(b_gk_w2[l], LANES),
                 row(b_gk_b[l]), row(b_norm_g[l]))
        h1, h1b = out_ln(ya.reshape(n, A_WIDTH), yb.reshape(n, B_WIDTH), h,
                         w_out[l][:A_WIDTH].astype(BF16), w_out[l][A_WIDTH:].astype(BF16),
                         row(ln1_g[l]), row(ln1_b[l]))
        ws = fold_keys(p_keys[l].reshape(P_HEADS * 2, P_KEYS, P_HALF), p_wq[l])
        e1, e2, gt = peer_topk(ws, h1b)
        h = peer_ffn(h1, h1b, e1.T, e2.T, gt.T, p_u[l].astype(BF16), p_v[l].astype(BF16),
                     row(ln2_g[l]), row(ln2_b[l]))
    return h.reshape(bsz, t, d)
```

```python
import functools
import math

import jax
import jax.numpy as jnp
from jax import lax
from jax.experimental import pallas as pl
from jax.experimental.pallas import tpu as pltpu

F32 = jnp.float32
BF16 = jnp.bfloat16
I32 = jnp.int32

D_MODEL = 1024
DEPTH = 1
CHUNK = 64
LN_EPS = 1e-5
ALPHA = (2.0 * DEPTH) ** 0.25

A_HEAD = 64
A_WIDTH = 512
A_HEADS = 8
A_GN_EPS = 64e-5
A_IN = 1792
A_PAD = 1920

B_HEADS = 4
B_DV = 128
B_DK = 64
B_WIDTH = 512
B_WIDTH_K = 256
B_QKV = 1024
B_GATE_LORA = 16
B_GATE_NORM = 16.0
B_CONV = 4
B_RMS_EPS = 1e-5
B_PAD = 1664

P_HEADS = 8
P_KEYS = 128
P_HALF = 128
P_TOPK = 16
P_EXPERTS = P_KEYS * P_KEYS

LANES = 128
G_PITCH = 72

VMEM_LIMIT = 56 * 1024 * 1024


def _dot(a, b):
    return jnp.dot(a.astype(BF16), b.astype(BF16), preferred_element_type=F32)


def _dot_nt(a, b):
    return lax.dot_general(a.astype(BF16), b.astype(BF16), (((1,), (1,)), ((), ())),
                           preferred_element_type=F32)


def _dot_tn(a, b):
    return lax.dot_general(a.astype(BF16), b.astype(BF16), (((0,), (0,)), ((), ())),
                           preferred_element_type=F32)


def _split3(x):
    hi = x.astype(BF16)
    r1 = x - hi.astype(F32)
    mid = r1.astype(BF16)
    lo = (r1 - mid.astype(F32)).astype(BF16)
    return hi, mid, lo


def _dot_exact_rhs(a01, x):
    a = a01.astype(BF16)
    hi, mid, lo = _split3(x)
    out = jnp.dot(a, hi, preferred_element_type=F32)
    out += jnp.dot(a, mid, preferred_element_type=F32)
    out += jnp.dot(a, lo, preferred_element_type=F32)
    return out


def _dot_exact_lhs(x, b01):
    b = b01.astype(BF16)
    hi, mid, lo = _split3(x)
    out = jnp.dot(hi, b, preferred_element_type=F32)
    out += jnp.dot(mid, b, preferred_element_type=F32)
    out += jnp.dot(lo, b, preferred_element_type=F32)
    return out


def _sigmoid(x):
    return 1.0 / (1.0 + jnp.exp(-x))


def _softplus(x):
    return jnp.maximum(x, 0.0) + jnp.log(1.0 + jnp.exp(-jnp.abs(x)))


def _silu(x):
    return x * _sigmoid(x)


def _layer_norm(x, g, b):
    mu = jnp.mean(x, axis=-1, keepdims=True)
    xc = x - mu
    var = jnp.mean(xc * xc, axis=-1, keepdims=True)
    return xc * lax.rsqrt(var + LN_EPS) * g + b


def _block_diag_ones(n, seg):
    r = lax.broadcasted_iota(I32, (n, n), 0) // seg
    c = lax.broadcasted_iota(I32, (n, n), 1) // seg
    return jnp.where(r == c, 1.0, 0.0).astype(F32)


def _chunk_tril(n, seg):
    r = lax.broadcasted_iota(I32, (n, n), 0)
    c = lax.broadcasted_iota(I32, (n, n), 1)
    return jnp.where((r // seg == c // seg) & (c <= r), 1.0, 0.0).astype(F32)


def _ln_proj_kernel(x_ref, g_ref, b_ref, wa_ref, wb_ref, h_ref, pa_ref, pb_ref):
    h = _layer_norm(x_ref[...], g_ref[...], b_ref[...])
    h_ref[...] = h
    hb = h.astype(BF16)
    pa_ref[...] = jnp.dot(hb, wa_ref[...], preferred_element_type=F32)
    pb_ref[...] = jnp.dot(hb, wb_ref[...], preferred_element_type=F32)


def ln_proj(x2, g, b, wa, wb, tm=512):
    n = x2.shape[0]
    full = lambda i: (0, 0)
    return pl.pallas_call(
        _ln_proj_kernel,
        grid=(n // tm,),
        in_specs=[
            pl.BlockSpec((tm, D_MODEL), lambda i: (i, 0)),
            pl.BlockSpec((1, D_MODEL), full),
            pl.BlockSpec((1, D_MODEL), full),
            pl.BlockSpec((D_MODEL, A_PAD), full),
            pl.BlockSpec((D_MODEL, B_PAD), full),
        ],
        out_specs=[
            pl.BlockSpec((tm, D_MODEL), lambda i: (i, 0)),
            pl.BlockSpec((tm, A_PAD), lambda i: (i, 0)),
            pl.BlockSpec((tm, B_PAD), lambda i: (i, 0)),
        ],
        out_shape=[
            jax.ShapeDtypeStruct((n, D_MODEL), F32),
            jax.ShapeDtypeStruct((n, A_PAD), F32),
            jax.ShapeDtypeStruct((n, B_PAD), F32),
        ],
        compiler_params=pltpu.CompilerParams(
            dimension_semantics=("arbitrary",), vmem_limit_bytes=VMEM_LIMIT),
        name="ln_proj",
    )(x2, g, b, wa, wb)


def _rwkv_kernel(pa_ref, mu_ref, w0_ref, w2_ref, a0_ref, a2_ref, g2_ref, kkw_ref, kaw_ref,
                 rk_ref, gng_ref, gnb_ref, ya_ref,
                 carry_ref, z_ref, at_s, rt_s, bt_s, kt_s, bh_s, kh_s, v_s, el_s, y_s, *, tb):
    nck = tb // CHUNK
    w = A_WIDTH

    @pl.when(pl.program_id(1) == 0)
    def _():
        carry_ref[...] = jnp.zeros_like(carry_ref)
        z_ref[...] = jnp.zeros_like(z_ref)

    p = pa_ref[0]
    rolled = pltpu.roll(p, 1, axis=0)
    row = lax.broadcasted_iota(I32, p.shape, 0)
    prev = jnp.where(row == 0, carry_ref[0:1, :], rolled)
    carry_ref[0:1, :] = p[tb - 1:tb, :]
    ps = p + (prev - p) * mu_ref[...]

    r = ps[:, 0:w]
    k = ps[:, w:2 * w]
    v = ps[:, 2 * w:3 * w]
    wd = ps[:, 3 * w:3 * w + LANES]
    ad = ps[:, 3 * w + LANES:3 * w + 2 * LANES]
    gd = ps[:, 3 * w + 2 * LANES:3 * w + 3 * LANES]

    wlin = w0_ref[...] + _dot(jnp.tanh(wd), w2_ref[...])
    lw = -jnp.exp(-_softplus(-wlin) - 0.5)
    iclr = _sigmoid(a0_ref[...] + _dot(ad, a2_ref[...]))
    gate = _dot(_sigmoid(gd), g2_ref[...])

    bd = _block_diag_ones(w, A_HEAD)
    kkr = k * kkw_ref[...]
    ss = _dot_exact_lhs(kkr * kkr, bd)
    kk = kkr / jnp.maximum(jnp.sqrt(ss), 1e-12)
    kh = k * (1.0 + (iclr - 1.0) * kaw_ref[...])
    b = kk * iclr
    bonus = _dot_exact_lhs(r * kh * rk_ref[...], bd) * v

    lcum = _dot_exact_rhs(_chunk_tril(tb, CHUNK), lw)
    e_l = jnp.exp(lcum)
    e_nl = jnp.exp(-lcum)
    at_s[...] = -kk * jnp.exp(lcum - lw)
    rt_s[...] = r * e_l
    bt_s[...] = b * e_nl
    kt_s[...] = kh * e_nl
    v_s[...] = v
    for c in range(nck):
        rows = slice(c * CHUNK, (c + 1) * CHUNK)
        l_end = lcum[(c + 1) * CHUNK - 1:(c + 1) * CHUNK, :]
        e_lc = jnp.exp(l_end - lcum[rows, :])
        bh_s[rows, :] = b[rows, :] * e_lc
        kh_s[rows, :] = kh[rows, :] * e_lc
        el_s[c:c + 1, :] = jnp.exp(l_end)

    ri = lax.broadcasted_iota(I32, (2 * CHUNK, 2 * CHUNK), 0)
    ci = lax.broadcasted_iota(I32, (2 * CHUNK, 2 * CHUNK), 1) % CHUNK
    tri_mask = ci < jnp.where(ri < CHUNK, ri, ri - (CHUNK - 1))
    er = lax.broadcasted_iota(I32, (CHUNK, CHUNK), 0)
    ec = lax.broadcasted_iota(I32, (CHUNK, CHUNK), 1)
    eye_mask = er == ec
    eye = jnp.where(eye_mask, 1.0, 0.0).astype(F32)

    def chunk_body(c, carry):
        r0 = pl.multiple_of(c * CHUNK, CHUNK)
        at_c = at_s[pl.ds(r0, CHUNK), :]
        rt_c = rt_s[pl.ds(r0, CHUNK), :]
        bt_c = bt_s[pl.ds(r0, CHUNK), :]
        kt_c = kt_s[pl.ds(r0, CHUNK), :]
        bh_c = bh_s[pl.ds(r0, CHUNK), :]
        kh_c = kh_s[pl.ds(r0, CHUNK), :]
        v_c = v_s[pl.ds(r0, CHUNK), :]
        el_c = el_s[pl.ds(c, 1), :]
        heads = range(A_HEADS)
        sls = [slice(h * A_HEAD, (h + 1) * A_HEAD) for h in heads]
        zs = [z_ref[h] for h in heads]
        gms = [jnp.where(tri_mask,
                         _dot_nt(jnp.concatenate([at_c[:, sl], rt_c[:, sl]], axis=0),
                                 jnp.concatenate([bt_c[:, sl], kt_c[:, sl]], axis=0)), 0.0)
               for sl in sls]
        mvs = [_dot(gm[:, CHUNK:2 * CHUNK], v_c[:, sl]) for gm, sl in zip(gms, sls)]
        khv = [_dot_tn(kh_c[:, sl], v_c[:, sl]) for sl in sls]
        pws = [gm[0:CHUNK, 0:CHUNK] for gm in gms]
        tms = [eye + pw for pw in pws]
        for _ in range(5):
            pws = [_dot(pw, pw) for pw in pws]
            tms = [tm + _dot(tm, pw) for tm, pw in zip(tms, pws)]
        p12s = [_dot(tm, jnp.concatenate([mv[0:CHUNK], at_c[:, sl]], axis=1))
                for tm, mv, sl in zip(tms, mvs, sls)]
        yqs = [jnp.concatenate([mv[CHUNK:2 * CHUNK], rt_c[:, sl]], axis=1)
               + _dot(gm[CHUNK:2 * CHUNK, 0:CHUNK], p12)
               for mv, gm, p12, sl in zip(mvs, gms, p12s, sls)]
        zws = [jnp.concatenate(
                   [kv, jnp.where(eye_mask, jnp.broadcast_to(el_c[:, sl], (CHUNK, CHUNK)), 0.0)], axis=1)
               + _dot_tn(bh_c[:, sl], p12)
               for kv, p12, sl in zip(khv, p12s, sls)]
        ys = [yq[:, 0:CHUNK] + _dot(yq[:, CHUNK:2 * CHUNK], z) for yq, z in zip(yqs, zs)]
        zn = [zw[:, 0:CHUNK] + _dot(zw[:, CHUNK:2 * CHUNK], z) for zw, z in zip(zws, zs)]
        for h in heads:
            z_ref[h] = zn[h]
        y_s[pl.ds(r0, CHUNK), :] = jnp.concatenate(ys, axis=1)
        return carry

    lax.fori_loop(0, nck, chunk_body, 0)

    y = y_s[...]
    inv_n = 1.0 / A_HEAD
    mean = _dot_exact_lhs(y, bd) * inv_n
    yc = y - mean
    var = _dot_exact_lhs(yc * yc, bd) * inv_n
    yn = yc * lax.rsqrt(var + A_GN_EPS) * gng_ref[...] + gnb_ref[...]
    ya_ref[0] = (yn + bonus) * gate


def rwkv7(pa3, mu, w0, w2, a0, a2, g2, kkw, kaw, rk, gng, gnb, tb=256):
    bsz, t, _ = pa3.shape
    full = lambda bi, ti: (0, 0)
    vec = pl.BlockSpec((1, A_WIDTH), full)
    slab = pltpu.VMEM((tb, A_WIDTH), F32)
    return pl.pallas_call(
        functools.partial(_rwkv_kernel, tb=tb),
        grid=(bsz, t // tb),
        in_specs=[
            pl.BlockSpec((1, tb, A_PAD), lambda bi, ti: (bi, ti, 0)),
            pl.BlockSpec((1, A_PAD), full),
            vec,
            pl.BlockSpec((LANES, A_WIDTH), full),
            vec,
            pl.BlockSpec((LANES, A_WIDTH), full),
            pl.BlockSpec((LANES, A_WIDTH), full),
            vec, vec, vec, vec, vec,
        ],
        out_specs=pl.BlockSpec((1, tb, A_WIDTH), lambda bi, ti: (bi, ti, 0)),
        out_shape=jax.ShapeDtypeStruct((bsz, t, A_WIDTH), F32),
        scratch_shapes=[
            pltpu.VMEM((8, A_PAD), F32),
            pltpu.VMEM((A_HEADS, A_HEAD, A_HEAD), F32),
            slab, slab, slab, slab, slab, slab, slab,
            pltpu.VMEM((8, A_WIDTH), F32),
            slab,
        ],
        compiler_params=pltpu.CompilerParams(
            dimension_semantics=("arbitrary", "arbitrary"), vmem_limit_bytes=VMEM_LIMIT),
        name="rwkv7",
    )(pa3, mu, w0, w2, a0, a2, g2, kkw, kaw, rk, gng, gnb)


def _gla_kernel(pb_ref, cw_ref, gw_ref, gb_ref, ng_ref, yb_ref,
                xpad_ref, s_ref, qt_s, kt_s, kh_s, v_s, gk_s, o_s, *, tb):
    nck = tb // CHUNK

    @pl.when(pl.program_id(1) == 0)
    def _():
        xpad_ref[0:8, :] = jnp.zeros((8, B_QKV), F32)
        s_ref[...] = jnp.zeros_like(s_ref)

    p = pb_ref[0]
    xpad_ref[8:8 + tb, :] = p[:, 0:B_QKV]
    conv = cw_ref[B_CONV - 1:B_CONV, :] * p[:, 0:B_QKV]
    for j in range(B_CONV - 1):
        sh = B_CONV - 1 - j
        conv = conv + cw_ref[j:j + 1, :] * xpad_ref[8 - sh:8 - sh + tb, :]
    xpad_ref[0:8, :] = p[tb - 8:tb, 0:B_QKV]
    qkv = _silu(conv)
    q = qkv[:, 0:B_WIDTH_K] * (B_DK ** -0.5)
    k = qkv[:, B_WIDTH_K:2 * B_WIDTH_K]
    v_s[...] = qkv[:, 2 * B_WIDTH_K:]
    gkd = p[:, B_QKV:B_QKV + LANES]
    gz = _dot(gkd, gw_ref[...]) + gb_ref[...]
    gk = -_softplus(-gz) * (1.0 / B_GATE_NORM)
    gk_s[...] = gk
    gcum = _dot_exact_rhs(_chunk_tril(tb, CHUNK), gk)
    qt_s[...] = q * jnp.exp(gcum)
    kt_s[...] = k * jnp.exp(-gcum)
    for c in range(nck):
        rows = slice(c * CHUNK, (c + 1) * CHUNK)
        g_end = gcum[(c + 1) * CHUNK - 1:(c + 1) * CHUNK, :]
        kh_s[rows, :] = k[rows, :] * jnp.exp(g_end - gcum[rows, :])

    ri = lax.broadcasted_iota(I32, (CHUNK, CHUNK), 0)
    ci = lax.broadcasted_iota(I32, (CHUNK, CHUNK), 1)
    causal = ci <= ri
    ones_cv = jnp.ones((CHUNK, B_DV), F32)

    def chunk_body(c, carry):
        r0 = pl.multiple_of(c * CHUNK, CHUNK)
        qt_c = qt_s[pl.ds(r0, CHUNK), :]
        kt_c = kt_s[pl.ds(r0, CHUNK), :]
        kh_c = kh_s[pl.ds(r0, CHUNK), :]
        v_c = v_s[pl.ds(r0, CHUNK), :]
        gk_c = gk_s[pl.ds(r0, CHUNK), :]
        os_ = []
        for h in range(B_HEADS):
            sk = slice(h * B_DK, (h + 1) * B_DK)
            sv = slice(h * B_DV, (h + 1) * B_DV)
            a = jnp.where(causal, _dot_nt(qt_c[:, sk], kt_c[:, sk]), 0.0)
            s = s_ref[h]
            os_.append(_dot(a, v_c[:, sv]) + _dot(qt_c[:, sk], s))
            hi, mid, lo = _split3(gk_c[:, sk])
            ob = ones_cv.astype(BF16)
            dn = (((0,), (0,)), ((), ()))
            gcol = (lax.dot_general(hi, ob, dn, preferred_element_type=F32)
                    + lax.dot_general(mid, ob, dn, preferred_element_type=F32)
                    + lax.dot_general(lo, ob, dn, preferred_element_type=F32))
            s_ref[h] = s * jnp.exp(gcol) + _dot_tn(kh_c[:, sk], v_c[:, sv])
        o_s[pl.ds(r0, CHUNK), :] = jnp.concatenate(os_, axis=1)
        return carry

    lax.fori_loop(0, nck, chunk_body, 0)

    g = p[:, B_QKV + LANES:]
    outs = []
    for h in range(B_HEADS):
        sv = slice(h * B_DV, (h + 1) * B_DV)
        o = o_s[:, sv]
        o = o * lax.rsqrt(jnp.mean(o * o, axis=-1, keepdims=True) + B_RMS_EPS) * ng_ref[...]
        outs.append(o * _silu(g[:, sv]))
    yb_ref[0] = jnp.concatenate(outs, axis=1)


def gla(pb3, cw, gw, gb, ng, tb=256):
    bsz, t, _ = pb3.shape
    full = lambda bi, ti: (0, 0)
    return pl.pallas_call(
        functools.partial(_gla_kernel, tb=tb),
        grid=(bsz, t // tb),
        in_specs=[
            pl.BlockSpec((1, tb, B_PAD), lambda bi, ti: (bi, ti, 0)),
            pl.BlockSpec((B_CONV, B_QKV), full),
            pl.BlockSpec((LANES, B_WIDTH_K), full),
            pl.BlockSpec((1, B_WIDTH_K), full),
            pl.BlockSpec((1, B_DV), full),
        ],
        out_specs=pl.BlockSpec((1, tb, B_WIDTH), lambda bi, ti: (bi, ti, 0)),
        out_shape=jax.ShapeDtypeStruct((bsz, t, B_WIDTH), F32),
        scratch_shapes=[
            pltpu.VMEM((tb + 8, B_QKV), F32),
            pltpu.VMEM((B_HEADS, B_DK, B_DV), F32),
            pltpu.VMEM((tb, B_WIDTH_K), F32),
            pltpu.VMEM((tb, B_WIDTH_K), F32),
            pltpu.VMEM((tb, B_WIDTH_K), F32),
            pltpu.VMEM((tb, B_WIDTH), F32),
            pltpu.VMEM((tb, B_WIDTH_K), F32),
            pltpu.VMEM((tb, B_WIDTH), F32),
        ],
        compiler_params=pltpu.CompilerParams(
            dimension_semantics=("arbitrary", "arbitrary"), vmem_limit_bytes=VMEM_LIMIT),
        name="gla",
    )(pb3, cw, gw, gb, ng)


def _out_ln_kernel(ya_ref, yb_ref, h_ref, wa_ref, wb_ref, g_ref, b_ref, h1_ref, h1b_ref):
    y = _dot(ya_ref[...], wa_ref[...]) + _dot(yb_ref[...], wb_ref[...])
    h1 = _layer_norm(ALPHA * h_ref[...] + y, g_ref[...], b_ref[...])
    h1_ref[...] = h1
    h1b_ref[...] = h1.astype(BF16)


def out_ln(ya, yb, h, wa, wb, g, b, tm=512):
    n = h.shape[0]
    full = lambda i: (0, 0)
    return pl.pallas_call(
        _out_ln_kernel,
        grid=(n // tm,),
        in_specs=[
            pl.BlockSpec((tm, A_WIDTH), lambda i: (i, 0)),
            pl.BlockSpec((tm, B_WIDTH), lambda i: (i, 0)),
            pl.BlockSpec((tm, D_MODEL), lambda i: (i, 0)),
            pl.BlockSpec((A_WIDTH, D_MODEL), full),
            pl.BlockSpec((B_WIDTH, D_MODEL), full),
            pl.BlockSpec((1, D_MODEL), full),
            pl.BlockSpec((1, D_MODEL), full),
        ],
        out_specs=[
            pl.BlockSpec((tm, D_MODEL), lambda i: (i, 0)),
            pl.BlockSpec((tm, D_MODEL), lambda i: (i, 0)),
        ],
        out_shape=[
            jax.ShapeDtypeStruct((n, D_MODEL), F32),
            jax.ShapeDtypeStruct((n, D_MODEL), BF16),
        ],
        compiler_params=pltpu.CompilerParams(
            dimension_semantics=("arbitrary",), vmem_limit_bytes=VMEM_LIMIT),
        name="out_ln",
    )(ya, yb, h, wa, wb, g, b)


def _fold_keys_kernel(keys_ref, wq_ref, out_ref):
    kh, km, kl = _split3(keys_ref[0])
    wh, wm, wl = _split3(wq_ref[...])
    dn = (((1,), (1,)), ((), ()))
    acc = lax.dot_general(kh, wh, dn, preferred_element_type=F32)
    for a, b in ((kh, wm), (km, wh), (km, wm), (kh, wl), (kl, wh)):
        acc += lax.dot_general(a, b, dn, preferred_element_type=F32)
    out_ref[...] = acc.astype(BF16)


def fold_keys(keys3, wq):
    nhp = keys3.shape[0]
    return pl.pallas_call(
        _fold_keys_kernel,
        grid=(nhp,),
        in_specs=[
            pl.BlockSpec((1, P_KEYS, P_HALF), lambda i: (i, 0, 0)),
            pl.BlockSpec((D_MODEL, P_HALF), lambda i: (0, i)),
        ],
        out_specs=pl.BlockSpec((P_KEYS, D_MODEL), lambda i: (i, 0)),
        out_shape=jax.ShapeDtypeStruct((nhp * P_KEYS, D_MODEL), BF16),
        compiler_params=pltpu.CompilerParams(dimension_semantics=("arbitrary",)),
        name="fold_keys",
    )(keys3, wq)


def _top16(vals, idx):
    out_v, out_i = [], []
    for _ in range(P_TOPK):
        m = jnp.max(vals, axis=0, keepdims=True)
        i = jnp.min(jnp.where(vals == m, idx, float(2 ** 20)), axis=0, keepdims=True)
        out_v.append(m)
        out_i.append(i)
        vals = jnp.where(idx == i, -jnp.inf, vals)
    return jnp.concatenate(out_v, axis=0), jnp.concatenate(out_i, axis=0)


def _cand_index(tt):
    r = lax.broadcasted_iota(I32, (72, tt), 0)
    c = jnp.where(r < 24, r, jnp.where(r < 32, r + 8, r + 16))
    c = jnp.where(r >= 40, (r - 40) * P_TOPK, c)
    c = jnp.where(r >= 56, (r - 56) * P_TOPK + 1, c)
    c = jnp.where(r >= 64, (r - 64) * P_TOPK + 2, c)
    pos = jnp.where(r >= 64, r - 64, jnp.where(r >= 56, r - 56, jnp.where(r >= 40, r - 40, 4)))
    return c.astype(F32), pos < 4


def _pick16(table, sel):
    out = jnp.zeros_like(table)
    for i in range(P_TOPK):
        out = jnp.where(sel == float(i), table[i:i + 1, :], out)
    return out


def _peer_topk_kernel(ws_ref, h_ref, e1_ref, e2_ref, gt_ref, s_ref):
    s_ref[...] = lax.dot_general(ws_ref[...], h_ref[...], (((1,), (1,)), ((), ())),
                                 preferred_element_type=F32)

    tt = s_ref.shape[1]
    key_idx = lax.broadcasted_iota(I32, (P_KEYS, tt), 0).astype(F32)
    cand_idx, cand_dup = _cand_index(tt)

    def head_body(h, carry):
        r0 = pl.multiple_of(h * 2 * P_KEYS, 2 * P_KEYS)
        sv1, si1 = _top16(s_ref[pl.ds(r0, P_KEYS), :], key_idx)
        sv2, si2 = _top16(s_ref[pl.ds(r0 + P_KEYS, P_KEYS), :], key_idx)
        cand = jnp.concatenate(
            [sv1[0:1, :] + sv2]
            + [sv1[i:i + 1, :] + sv2[0:8, :] for i in (1, 2, 3)]
            + [sv1 + sv2[0:1, :]]
            + [sv1[0:8, :] + sv2[j:j + 1, :] for j in (1, 2)], axis=0)
        cv, cidx = _top16(jnp.where(cand_dup, -jnp.inf, cand), cand_idx)
        qi = jnp.floor(cidx * (1.0 / P_TOPK))
        ri = cidx - qi * P_TOPK
        e1 = _pick16(si1, qi)
        e2 = _pick16(si2, ri)
        ex = jnp.exp(cv - cv[0:1, :])
        gates = ex / jnp.sum(ex, axis=0, keepdims=True)
        o0 = pl.multiple_of(h * P_TOPK, P_TOPK)
        e1_ref[pl.ds(o0, P_TOPK), :] = e1.astype(I32)
        e2_ref[pl.ds(o0, P_TOPK), :] = e2.astype(I32)
        gt_ref[pl.ds(o0, P_TOPK), :] = gates
        return carry

    lax.fori_loop(0, P_HEADS, head_body, 0)


def peer_topk(ws, h1b, tt=256):
    n = h1b.shape[0]
    nsel = P_HEADS * P_TOPK
    nrow = P_HEADS * 2 * P_KEYS
    out = jax.ShapeDtypeStruct((nsel, n), I32)
    spec = pl.BlockSpec((nsel, tt), lambda i: (0, i))
    return pl.pallas_call(
        _peer_topk_kernel,
        grid=(n // tt,),
        in_specs=[
            pl.BlockSpec((nrow, D_MODEL), lambda i: (0, 0)),
            pl.BlockSpec((tt, D_MODEL), lambda i: (i, 0)),
        ],
        out_specs=[spec, spec, spec],
        out_shape=[out, out, jax.ShapeDtypeStruct((nsel, n), F32)],
        scratch_shapes=[pltpu.VMEM((nrow, tt), F32)],
        compiler_params=pltpu.CompilerParams(
            dimension_semantics=("arbitrary",), vmem_limit_bytes=VMEM_LIMIT),
        name="peer_topk",
    )(ws, h1b)


def _gelu(x):
    return 0.5 * x * (1.0 + lax.erf(x * (1.0 / math.sqrt(2.0))))


def _peer_ffn_kernel(h1_ref, h1b_ref, e1_ref, e2_ref, gt_ref, ulo_ref, uhi_ref, vlo_ref, vhi_ref,
                     g_ref, b_ref, out_ref, gs_ref, acc_ref, *, tt, ngrp):
    j = pl.program_id(1)
    nsel = P_HEADS * P_TOPK
    half = P_KEYS // 2

    @pl.when(j == 0)
    def _():
        acc_ref[...] = jnp.zeros_like(acc_ref)
        sub = lax.broadcasted_iota(I32, (P_KEYS, nsel), 0)
        zero = jnp.zeros((P_KEYS, nsel), F32)

        def pair_body(m, carry):
            n0 = 2 * m
            onehots = []
            gated = []
            for q in range(2):
                e1 = e1_ref[pl.ds(n0 + q, 1), :]
                e2 = e2_ref[pl.ds(n0 + q, 1), :]
                gt = gt_ref[pl.ds(n0 + q, 1), :]
                gated.append(jnp.where(sub == e1, gt, 0.0))
                onehots.append(jnp.where(sub == e2, 1.0, 0.0))
            a_cat = jnp.concatenate(gated, axis=1).astype(BF16)
            b_bd = jnp.concatenate(
                [jnp.concatenate([onehots[0], zero], axis=1),
                 jnp.concatenate([zero, onehots[1]], axis=1)], axis=0).astype(BF16)
            gn = lax.dot_general(a_cat, b_bd, (((1,), (1,)), ((), ())),
                                 preferred_element_type=F32)
            bits = pltpu.bitcast(gn, I32) + 0x8000
            packed = (bits[half:, :] & -65536) | lax.shift_right_logical(bits[:half, :], 16)
            for q in range(2):
                r0 = pl.multiple_of((n0 + q) * G_PITCH, 8)
                gs_ref[pl.ds(r0, half), :] = packed[:, q * P_KEYS:(q + 1) * P_KEYS]
            return carry

        lax.fori_loop(0, tt // 2, pair_body, 0, unroll=16)

    x = h1b_ref[...]
    sub = 2 * P_KEYS
    dn = (((1,), (1,)), ((), ()))
    h_lo, h_hi = [], []
    for i in range(ngrp // 2):
        words = [gs_ref[pl.ds(j * ngrp + 2 * i + q, tt, stride=G_PITCH), :] for q in range(2)]
        g_lo = jnp.concatenate([pltpu.bitcast(w << 16, F32) for w in words], axis=1)
        g_hi = jnp.concatenate([pltpu.bitcast(w & -65536, F32) for w in words], axis=1)
        rows = slice(i * sub, (i + 1) * sub)
        s_lo = lax.dot_general(x, ulo_ref[rows, :], dn, preferred_element_type=F32)
        s_hi = lax.dot_general(x, uhi_ref[rows, :], dn, preferred_element_type=F32)
        h_lo.append((g_lo * _gelu(s_lo)).astype(BF16))
        h_hi.append((g_hi * _gelu(s_hi)).astype(BF16))
    acc = acc_ref[...]
    acc = acc + jnp.dot(jnp.concatenate(h_lo, axis=1), vlo_ref[...], preferred_element_type=F32)
    acc = acc + jnp.dot(jnp.concatenate(h_hi, axis=1), vhi_ref[...], preferred_element_type=F32)
    acc_ref[...] = acc

    @pl.when(j == pl.num_programs(1) - 1)
    def _():
        out_ref[...] = _layer_norm(ALPHA * h1_ref[...] + acc_ref[...], g_ref[...], b_ref[...])


def peer_ffn(h1, h1b, e1t, e2t, gtt, ub, vb, g, b, tt=512, ngrp=8):
    n = h1.shape[0]
    nsel = P_HEADS * P_TOPK
    nstep = P_KEYS // 2 // ngrp
    rows = ngrp * P_KEYS
    tok = lambda i, j: (i, 0)
    full = lambda i, j: (0, 0)
    lo = lambda i, j: (j, 0)
    hi = lambda i, j: (j + nstep, 0)
    once = pl.Buffered(1)
    return pl.pallas_call(
        functools.partial(_peer_ffn_kernel, tt=tt, ngrp=ngrp),
        grid=(n // tt, nstep),
        in_specs=[
            pl.BlockSpec((tt, D_MODEL), tok, pipeline_mode=once),
            pl.BlockSpec((tt, D_MODEL), tok, pipeline_mode=once),
            pl.BlockSpec((tt, nsel), tok, pipeline_mode=once),
            pl.BlockSpec((tt, nsel), tok, pipeline_mode=once),
            pl.BlockSpec((tt, nsel), tok, pipeline_mode=once),
            pl.BlockSpec((rows, D_MODEL), lo),
            pl.BlockSpec((rows, D_MODEL), hi),
            pl.BlockSpec((rows, D_MODEL), lo),
            pl.BlockSpec((rows, D_MODEL), hi),
            pl.BlockSpec((1, D_MODEL), full),
            pl.BlockSpec((1, D_MODEL), full),
        ],
        out_specs=pl.BlockSpec((tt, D_MODEL), tok),
        out_shape=jax.ShapeDtypeStruct((n, D_MODEL), F32),
        scratch_shapes=[
            pltpu.VMEM((tt * G_PITCH, P_KEYS), I32),
            pltpu.VMEM((tt, D_MODEL), F32),
        ],
        compiler_params=pltpu.CompilerParams(
            dimension_semantics=("arbitrary", "arbitrary"), vmem_limit_bytes=VMEM_LIMIT),
        name="peer_ffn",
    )(h1, h1b, e1t, e2t, gtt, ub, ub, vb, vb, g, b)


def _pad_cols(w, width):
    return jnp.pad(w, ((0, 0), (0, width - w.shape[1])))


def _pad_rows(w, height):
    return jnp.pad(w, ((0, height - w.shape[0]), (0, 0)))


def _layout_a(w):
    w3 = 3 * A_WIDTH
    return jnp.concatenate([
        w[:, 0:w3], _pad_cols(w[:, w3:w3 + 64], LANES), _pad_cols(w[:, w3 + 64:w3 + 128], LANES),
        w[:, w3 + 128:A_IN]], axis=1)


def _layout_b(w):
    return jnp.concatenate([
        w[:, 0:B_QKV], _pad_cols(w[:, B_QKV:B_QKV + B_GATE_LORA], LANES),
        w[:, B_QKV + B_GATE_LORA:]], axis=1)


def kernel(x, ln_in_g, ln_in_b, w_in, a_mu, a_w0, a_w2, a_a0, a_a2, a_g2, a_k_k, a_k_a, a_r_k,
           a_gn_g, a_gn_b, b_conv, b_gk_w2, b_gk_b, b_norm_g, w_out, ln1_g, ln1_b, p_wq, p_keys,
           p_u, p_v, ln2_g, ln2_b):
    bsz, t, d = x.shape
    n = bsz * t
    row = lambda a: a.reshape(1, -1)

    h = x.reshape(n, d)
    for l in range(DEPTH):
        wa = _layout_a(w_in[l][:, :A_IN]).astype(BF16)
        wb = _layout_b(w_in[l][:, A_IN:]).astype(BF16)
        if l == 0:
            h, pa, pb = ln_proj(h, row(ln_in_g), row(ln_in_b), wa, wb)
        ya = rwkv7(
            pa.reshape(bsz, t, A_PAD), _layout_a(row(a_mu[l])), row(a_w0[l]),
            _pad_rows(a_w2[l], LANES), row(a_a0[l]), _pad_rows(a_a2[l], LANES), a_g2[l],
            row(a_k_k[l]), row(a_k_a[l]), row(a_r_k[l]), row(a_gn_g[l]), row(a_gn_b[l]))
        yb = gla(pb.reshape(bsz, t, B_PAD), b_conv[l], _pad_rows(b_gk_w2[l], LANES),
                 row(b_gk_b[l]), row(b_norm_g[l]))
        h1, h1b = out_ln(ya.reshape(n, A_WIDTH), yb.reshape(n, B_WIDTH), h,
                         w_out[l][:A_WIDTH].astype(BF16), w_out[l][A_WIDTH:].astype(BF16),
                         row(ln1_g[l]), row(ln1_b[l]))
        ws = fold_keys(p_keys[l].reshape(P_HEADS * 2, P_KEYS, P_HALF), p_wq[l])
        e1, e2, gt = peer_topk(ws, h1b)
        h = peer_ffn(h1, h1b, e1.T, e2.T, gt.T, p_u[l].astype(BF16), p_v[l].astype(BF16),
                     row(ln2_g[l]), row(ln2_b[l]))
    return h.reshape(bsz, t, d)
```

```python
import functools
import math

import jax
import jax.numpy as jnp
from jax import lax
from jax.experimental import pallas as pl
from jax.experimental.pallas import tpu as pltpu

F32 = jnp.float32
BF16 = jnp.bfloat16
I32 = jnp.int32

D_MODEL = 1024
DEPTH = 1
CHUNK = 64
LN_EPS = 1e-5
ALPHA = (2.0 * DEPTH) ** 0.25

A_HEAD = 64
A_WIDTH = 512
A_HEADS = 8
A_GN_EPS = 64e-5
A_IN = 1792
A_PAD = 1920

B_HEADS = 4
B_DV = 128
B_DK = 64
B_WIDTH = 512
B_WIDTH_K = 256
B_QKV = 1024
B_GATE_LORA = 16
B_GATE_NORM = 16.0
B_CONV = 4
B_RMS_EPS = 1e-5
B_PAD = 1664

P_HEADS = 8
P_KEYS = 128
P_HALF = 128
P_TOPK = 16
P_EXPERTS = P_KEYS * P_KEYS

LANES = 128
G_PITCH = 72

VMEM_LIMIT = 56 * 1024 * 1024


def _dot(a, b):
    return jnp.dot(a.astype(BF16), b.astype(BF16), preferred_element_type=F32)


def _dot_nt(a, b):
    return lax.dot_general(a.astype(BF16), b.astype(BF16), (((1,), (1,)), ((), ())),
                           preferred_element_type=F32)


def _dot_tn(a, b):
    return lax.dot_general(a.astype(BF16), b.astype(BF16), (((0,), (0,)), ((), ())),
                           preferred_element_type=F32)


def _split3(x):
    hi = x.astype(BF16)
    r1 = x - hi.astype(F32)
    mid = r1.astype(BF16)
    lo = (r1 - mid.astype(F32)).astype(BF16)
    return hi, mid, lo


def _dot_exact_rhs(a01, x):
    a = a01.astype(BF16)
    hi, mid, lo = _split3(x)
    out = jnp.dot(a, hi, preferred_element_type=F32)
    out += jnp.dot(a, mid, preferred_element_type=F32)
    out += jnp.dot(a, lo, preferred_element_type=F32)
    return out


def _dot_exact_lhs(x, b01):
    b = b01.astype(BF16)
    hi, mid, lo = _split3(x)
    out = jnp.dot(hi, b, preferred_element_type=F32)
    out += jnp.dot(mid, b, preferred_element_type=F32)
    out += jnp.dot(lo, b, preferred_element_type=F32)
    return out


def _sigmoid(x):
    return 1.0 / (1.0 + jnp.exp(-x))


def _softplus(x):
    return jnp.maximum(x, 0.0) + jnp.log(1.0 + jnp.exp(-jnp.abs(x)))


def _silu(x):
    return x * _sigmoid(x)


def _layer_norm(x, g, b):
    mu = jnp.mean(x, axis=-1, keepdims=True)
    xc = x - mu
    var = jnp.mean(xc * xc, axis=-1, keepdims=True)
    return xc * lax.rsqrt(var + LN_EPS) * g + b


def _block_diag_ones(n, seg):
    r = lax.broadcasted_iota(I32, (n, n), 0) // seg
    c = lax.broadcasted_iota(I32, (n, n), 1) // seg
    return jnp.where(r == c, 1.0, 0.0).astype(F32)


def _chunk_tril(n, seg):
    r = lax.broadcasted_iota(I32, (n, n), 0)
    c = lax.broadcasted_iota(I32, (n, n), 1)
    return jnp.where((r // seg == c // seg) & (c <= r), 1.0, 0.0).astype(F32)


def _ln_proj_kernel(x_ref, g_ref, b_ref, wa_ref, wb_ref, h_ref, pa_ref, pb_ref):
    h = _layer_norm(x_ref[...], g_ref[...], b_ref[...])
    h_ref[...] = h
    hb = h.astype(BF16)
    pa_ref[...] = jnp.dot(hb, wa_ref[...], preferred_element_type=F32)
    pb_ref[...] = jnp.dot(hb, wb_ref[...], preferred_element_type=F32)


def ln_proj(x2, g, b, wa, wb, tm=512):
    n = x2.shape[0]
    full = lambda i: (0, 0)
    return pl.pallas_call(
        _ln_proj_kernel,
        grid=(n // tm,),
        in_specs=[
            pl.BlockSpec((tm, D_MODEL), lambda i: (i, 0)),
            pl.BlockSpec((1, D_MODEL), full),
            pl.BlockSpec((1, D_MODEL), full),
            pl.BlockSpec((D_MODEL, A_PAD), full),
            pl.BlockSpec((D_MODEL, B_PAD), full),
        ],
        out_specs=[
            pl.BlockSpec((tm, D_MODEL), lambda i: (i, 0)),
            pl.BlockSpec((tm, A_PAD), lambda i: (i, 0)),
            pl.BlockSpec((tm, B_PAD), lambda i: (i, 0)),
        ],
        out_shape=[
            jax.ShapeDtypeStruct((n, D_MODEL), F32),
            jax.ShapeDtypeStruct((n, A_PAD), F32),
            jax.ShapeDtypeStruct((n, B_PAD), F32),
        ],
        compiler_params=pltpu.CompilerParams(
            dimension_semantics=("arbitrary",), vmem_limit_bytes=VMEM_LIMIT),
        name="ln_proj",
    )(x2, g, b, wa, wb)


def _rwkv_kernel(pa_ref, mu_ref, w0_ref, w2_ref, a0_ref, a2_ref, g2_ref, kkw_ref, kaw_ref,
                 rk_ref, gng_ref, gnb_ref, ya_ref,
                 carry_ref, z_ref, at_s, rt_s, bt_s, kt_s, bh_s, kh_s, v_s, el_s, y_s, *, tb):
    nck = tb // CHUNK
    w = A_WIDTH

    @pl.when(pl.program_id(1) == 0)
    def _():
        carry_ref[...] = jnp.zeros_like(carry_ref)
        z_ref[...] = jnp.zeros_like(z_ref)

    p = pa_ref[0]
    rolled = pltpu.roll(p, 1, axis=0)
    row = lax.broadcasted_iota(I32, p.shape, 0)
    prev = jnp.where(row == 0, carry_ref[0:1, :], rolled)
    carry_ref[0:1, :] = p[tb - 1:tb, :]
    ps = p + (prev - p) * mu_ref[...]

    r = ps[:, 0:w]
    k = ps[:, w:2 * w]
    v = ps[:, 2 * w:3 * w]
    wd = ps[:, 3 * w:3 * w + LANES]
    ad = ps[:, 3 * w + LANES:3 * w + 2 * LANES]
    gd = ps[:, 3 * w + 2 * LANES:3 * w + 3 * LANES]

    wlin = w0_ref[...] + _dot(jnp.tanh(wd), w2_ref[...])
    lw = -jnp.exp(-_softplus(-wlin) - 0.5)
    iclr = _sigmoid(a0_ref[...] + _dot(ad, a2_ref[...]))
    gate = _dot(_sigmoid(gd), g2_ref[...])

    bd = _block_diag_ones(w, A_HEAD)
    kkr = k * kkw_ref[...]
    ss = _dot_exact_lhs(kkr * kkr, bd)
    kk = kkr / jnp.maximum(jnp.sqrt(ss), 1e-12)
    kh = k * (1.0 + (iclr - 1.0) * kaw_ref[...])
    b = kk * iclr
    bonus = _dot_exact_lhs(r * kh * rk_ref[...], bd) * v

    lcum = _dot_exact_rhs(_chunk_tril(tb, CHUNK), lw)
    e_l = jnp.exp(lcum)
    e_nl = jnp.exp(-lcum)
    at_s[...] = -kk * jnp.exp(lcum - lw)
    rt_s[...] = r * e_l
    bt_s[...] = b * e_nl
    kt_s[...] = kh * e_nl
    v_s[...] = v
    for c in range(nck):
        rows = slice(c * CHUNK, (c + 1) * CHUNK)
        l_end = lcum[(c + 1) * CHUNK - 1:(c + 1) * CHUNK, :]
        e_lc = jnp.exp(l_end - lcum[rows, :])
        bh_s[rows, :] = b[rows, :] * e_lc
        kh_s[rows, :] = kh[rows, :] * e_lc
        el_s[c:c + 1, :] = jnp.exp(l_end)

    ri = lax.broadcasted_iota(I32, (2 * CHUNK, 2 * CHUNK), 0)
    ci = lax.broadcasted_iota(I32, (2 * CHUNK, 2 * CHUNK), 1) % CHUNK
    tri_mask = ci < jnp.where(ri < CHUNK, ri, ri - (CHUNK - 1))
    er = lax.broadcasted_iota(I32, (CHUNK, CHUNK), 0)
    ec = lax.broadcasted_iota(I32, (CHUNK, CHUNK), 1)
    eye_mask = er == ec
    eye = jnp.where(eye_mask, 1.0, 0.0).astype(F32)

    heads = range(A_HEADS)
    chains = [(c, h) for c in range(nck) for h in heads]
    rows = [slice(c * CHUNK, (c + 1) * CHUNK) for c in range(nck)]
    slabs = [{name: ref[rows[c], :] for name, ref in
              (("at", at_s), ("rt", rt_s), ("bt", bt_s), ("kt", kt_s), ("bh", bh_s), ("kh", kh_s), ("v", v_s))}
             for c in range(nck)]
    els = [el_s[c:c + 1, :] for c in range(nck)]

    def part(name, c, h):
        return slabs[c][name][:, h * A_HEAD:(h + 1) * A_HEAD]

    gms = [jnp.where(tri_mask,
                     _dot_nt(jnp.concatenate([part("at", c, h), part("rt", c, h)], axis=0),
                             jnp.concatenate([part("bt", c, h), part("kt", c, h)], axis=0)), 0.0)
           for c, h in chains]
    mvs = [_dot(gm[:, CHUNK:2 * CHUNK], part("v", c, h)) for gm, (c, h) in zip(gms, chains)]
    khv = [_dot_tn(part("kh", c, h), part("v", c, h)) for c, h in chains]
    pws = [gm[0:CHUNK, 0:CHUNK] for gm in gms]
    tms = [eye + pw for pw in pws]
    for _ in range(5):
        pws = [_dot(pw, pw) for pw in pws]
        tms = [tm + _dot(tm, pw) for tm, pw in zip(tms, pws)]
    p12s = [_dot(tm, jnp.concatenate([mv[0:CHUNK], part("at", c, h)], axis=1))
            for tm, mv, (c, h) in zip(tms, mvs, chains)]
    yqs = [jnp.concatenate([mv[CHUNK:2 * CHUNK], part("rt", c, h)], axis=1)
           + _dot(gm[CHUNK:2 * CHUNK, 0:CHUNK], p12)
           for mv, gm, p12, (c, h) in zip(mvs, gms, p12s, chains)]
    zws = [jnp.concatenate(
               [kv, jnp.where(eye_mask,
                              jnp.broadcast_to(els[c][:, h * A_HEAD:(h + 1) * A_HEAD], (CHUNK, CHUNK)), 0.0)],
               axis=1)
           + _dot_tn(part("bh", c, h), p12)
           for kv, p12, (c, h) in zip(khv, p12s, chains)]

    zs = [z_ref[h] for h in heads]
    for c in range(nck):
        ys = []
        for h in heads:
            yq, zw = yqs[c * A_HEADS + h], zws[c * A_HEADS + h]
            ys.append(yq[:, 0:CHUNK] + _dot(yq[:, CHUNK:2 * CHUNK], zs[h]))
            zs[h] = zw[:, 0:CHUNK] + _dot(zw[:, CHUNK:2 * CHUNK], zs[h])
        y_s[rows[c], :] = jnp.concatenate(ys, axis=1)
    for h in heads:
        z_ref[h] = zs[h]

    y = y_s[...]
    inv_n = 1.0 / A_HEAD
    mean = _dot_exact_lhs(y, bd) * inv_n
    yc = y - mean
    var = _dot_exact_lhs(yc * yc, bd) * inv_n
    yn = yc * lax.rsqrt(var + A_GN_EPS) * gng_ref[...] + gnb_ref[...]
    ya_ref[0] = (yn + bonus) * gate


def rwkv7(pa3, mu, w0, w2, a0, a2, g2, kkw, kaw, rk, gng, gnb, tb=256):
    bsz, t, _ = pa3.shape
    full = lambda bi, ti: (0, 0)
    vec = pl.BlockSpec((1, A_WIDTH), full)
    slab = pltpu.VMEM((tb, A_WIDTH), F32)
    return pl.pallas_call(
        functools.partial(_rwkv_kernel, tb=tb),
        grid=(bsz, t // tb),
        in_specs=[
            pl.BlockSpec((1, tb, A_PAD), lambda bi, ti: (bi, ti, 0)),
            pl.BlockSpec((1, A_PAD), full),
            vec,
            pl.BlockSpec((LANES, A_WIDTH), full),
            vec,
            pl.BlockSpec((LANES, A_WIDTH), full),
            pl.BlockSpec((LANES, A_WIDTH), full),
            vec, vec, vec, vec, vec,
        ],
        out_specs=pl.BlockSpec((1, tb, A_WIDTH), lambda bi, ti: (bi, ti, 0)),
        out_shape=jax.ShapeDtypeStruct((bsz, t, A_WIDTH), F32),
        scratch_shapes=[
            pltpu.VMEM((8, A_PAD), F32),
            pltpu.VMEM((A_HEADS, A_HEAD, A_HEAD), F32),
            slab, slab, slab, slab, slab, slab, slab,
            pltpu.VMEM((8, A_WIDTH), F32),
            slab,
        ],
        compiler_params=pltpu.CompilerParams(
            dimension_semantics=("arbitrary", "arbitrary"), vmem_limit_bytes=VMEM_LIMIT),
        name="rwkv7",
    )(pa3, mu, w0, w2, a0, a2, g2, kkw, kaw, rk, gng, gnb)


def _gla_kernel(pb_ref, cw_ref, gw_ref, gb_ref, ng_ref, yb_ref,
                xpad_ref, s_ref, qt_s, kt_s, kh_s, v_s, gk_s, o_s, *, tb):
    nck = tb // CHUNK

    @pl.when(pl.program_id(1) == 0)
    def _():
        xpad_ref[0:8, :] = jnp.zeros((8, B_QKV), F32)
        s_ref[...] = jnp.zeros_like(s_ref)

    p = pb_ref[0]
    xpad_ref[8:8 + tb, :] = p[:, 0:B_QKV]
    conv = cw_ref[B_CONV - 1:B_CONV, :] * p[:, 0:B_QKV]
    for j in range(B_CONV - 1):
        sh = B_CONV - 1 - j
        conv = conv + cw_ref[j:j + 1, :] * xpad_ref[8 - sh:8 - sh + tb, :]
    xpad_ref[0:8, :] = p[tb - 8:tb, 0:B_QKV]
    qkv = _silu(conv)
    q = qkv[:, 0:B_WIDTH_K] * (B_DK ** -0.5)
    k = qkv[:, B_WIDTH_K:2 * B_WIDTH_K]
    v_s[...] = qkv[:, 2 * B_WIDTH_K:]
    gkd = p[:, B_QKV:B_QKV + LANES]
    gz = _dot(gkd, gw_ref[...]) + gb_ref[...]
    gk = -_softplus(-gz) * (1.0 / B_GATE_NORM)
    gk_s[...] = gk
    gcum = _dot_exact_rhs(_chunk_tril(tb, CHUNK), gk)
    qt_s[...] = q * jnp.exp(gcum)
    kt_s[...] = k * jnp.exp(-gcum)
    for c in range(nck):
        rows = slice(c * CHUNK, (c + 1) * CHUNK)
        g_end = gcum[(c + 1) * CHUNK - 1:(c + 1) * CHUNK, :]
        kh_s[rows, :] = k[rows, :] * jnp.exp(g_end - gcum[rows, :])

    ri = lax.broadcasted_iota(I32, (CHUNK, CHUNK), 0)
    ci = lax.broadcasted_iota(I32, (CHUNK, CHUNK), 1)
    causal = ci <= ri
    ones_cv = jnp.ones((CHUNK, B_DV), F32)

    def chunk_body(c, carry):
        r0 = pl.multiple_of(c * CHUNK, CHUNK)
        qt_c = qt_s[pl.ds(r0, CHUNK), :]
        kt_c = kt_s[pl.ds(r0, CHUNK), :]
        kh_c = kh_s[pl.ds(r0, CHUNK), :]
        v_c = v_s[pl.ds(r0, CHUNK), :]
        gk_c = gk_s[pl.ds(r0, CHUNK), :]
        os_ = []
        for h in range(B_HEADS):
            sk = slice(h * B_DK, (h + 1) * B_DK)
            sv = slice(h * B_DV, (h + 1) * B_DV)
            a = jnp.where(causal, _dot_nt(qt_c[:, sk], kt_c[:, sk]), 0.0)
            s = s_ref[h]
            os_.append(_dot(a, v_c[:, sv]) + _dot(qt_c[:, sk], s))
            hi, mid, lo = _split3(gk_c[:, sk])
            ob = ones_cv.astype(BF16)
            dn = (((0,), (0,)), ((), ()))
            gcol = (lax.dot_general(hi, ob, dn, preferred_element_type=F32)
                    + lax.dot_general(mid, ob, dn, preferred_element_type=F32)
                    + lax.dot_general(lo, ob, dn, preferred_element_type=F32))
            s_ref[h] = s * jnp.exp(gcol) + _dot_tn(kh_c[:, sk], v_c[:, sv])
        o_s[pl.ds(r0, CHUNK), :] = jnp.concatenate(os_, axis=1)
        return carry

    lax.fori_loop(0, nck, chunk_body, 0)

    g = p[:, B_QKV + LANES:]
    outs = []
    for h in range(B_HEADS):
        sv = slice(h * B_DV, (h + 1) * B_DV)
        o = o_s[:, sv]
        o = o * lax.rsqrt(jnp.mean(o * o, axis=-1, keepdims=True) + B_RMS_EPS) * ng_ref[...]
        outs.append(o * _silu(g[:, sv]))
    yb_ref[0] = jnp.concatenate(outs, axis=1)


def gla(pb3, cw, gw, gb, ng, tb=256):
    bsz, t, _ = pb3.shape
    full = lambda bi, ti: (0, 0)
    return pl.pallas_call(
        functools.partial(_gla_kernel, tb=tb),
        grid=(bsz, t // tb),
        in_specs=[
            pl.BlockSpec((1, tb, B_PAD), lambda bi, ti: (bi, ti, 0)),
            pl.BlockSpec((B_CONV, B_QKV), full),
            pl.BlockSpec((LANES, B_WIDTH_K), full),
            pl.BlockSpec((1, B_WIDTH_K), full),
            pl.BlockSpec((1, B_DV), full),
        ],
        out_specs=pl.BlockSpec((1, tb, B_WIDTH), lambda bi, ti: (bi, ti, 0)),
        out_shape=jax.ShapeDtypeStruct((bsz, t, B_WIDTH), F32),
        scratch_shapes=[
            pltpu.VMEM((tb + 8, B_QKV), F32),
            pltpu.VMEM((B_HEADS, B_DK, B_DV), F32),
            pltpu.VMEM((tb, B_WIDTH_K), F32),
            pltpu.VMEM((tb, B_WIDTH_K), F32),
            pltpu.VMEM((tb, B_WIDTH_K), F32),
            pltpu.VMEM((tb, B_WIDTH), F32),
            pltpu.VMEM((tb, B_WIDTH_K), F32),
            pltpu.VMEM((tb, B_WIDTH), F32),
        ],
        compiler_params=pltpu.CompilerParams(
            dimension_semantics=("arbitrary", "arbitrary"), vmem_limit_bytes=VMEM_LIMIT),
        name="gla",
    )(pb3, cw, gw, gb, ng)


def _out_ln_kernel(ya_ref, yb_ref, h_ref, wa_ref, wb_ref, g_ref, b_ref, h1_ref, h1b_ref):
    y = _dot(ya_ref[...], wa_ref[...]) + _dot(yb_ref[...], wb_ref[...])
    h1 = _layer_norm(ALPHA * h_ref[...] + y, g_ref[...], b_ref[...])
    h1_ref[...] = h1
    h1b_ref[...] = h1.astype(BF16)


def out_ln(ya, yb, h, wa, wb, g, b, tm=512):
    n = h.shape[0]
    full = lambda i: (0, 0)
    return pl.pallas_call(
        _out_ln_kernel,
        grid=(n // tm,),
        in_specs=[
            pl.BlockSpec((tm, A_WIDTH), lambda i: (i, 0)),
            pl.BlockSpec((tm, B_WIDTH), lambda i: (i, 0)),
            pl.BlockSpec((tm, D_MODEL), lambda i: (i, 0)),
            pl.BlockSpec((A_WIDTH, D_MODEL), full),
            pl.BlockSpec((B_WIDTH, D_MODEL), full),
            pl.BlockSpec((1, D_MODEL), full),
            pl.BlockSpec((1, D_MODEL), full),
        ],
        out_specs=[
            pl.BlockSpec((tm, D_MODEL), lambda i: (i, 0)),
            pl.BlockSpec((tm, D_MODEL), lambda i: (i, 0)),
        ],
        out_shape=[
            jax.ShapeDtypeStruct((n, D_MODEL), F32),
            jax.ShapeDtypeStruct((n, D_MODEL), BF16),
        ],
        compiler_params=pltpu.CompilerParams(
            dimension_semantics=("arbitrary",), vmem_limit_bytes=VMEM_LIMIT),
        name="out_ln",
    )(ya, yb, h, wa, wb, g, b)


def _fold_keys_kernel(keys_ref, wq_ref, out_ref):
    kh, km, kl = _split3(keys_ref[0])
    wh, wm, wl = _split3(wq_ref[...])
    dn = (((1,), (1,)), ((), ()))
    acc = lax.dot_general(kh, wh, dn, preferred_element_type=F32)
    for a, b in ((kh, wm), (km, wh), (km, wm), (kh, wl), (kl, wh)):
        acc += lax.dot_general(a, b, dn, preferred_element_type=F32)
    out_ref[...] = acc.astype(BF16)


def fold_keys(keys3, wq):
    nhp = keys3.shape[0]
    return pl.pallas_call(
        _fold_keys_kernel,
        grid=(nhp,),
        in_specs=[
            pl.BlockSpec((1, P_KEYS, P_HALF), lambda i: (i, 0, 0)),
            pl.BlockSpec((D_MODEL, P_HALF), lambda i: (0, i)),
        ],
        out_specs=pl.BlockSpec((P_KEYS, D_MODEL), lambda i: (i, 0)),
        out_shape=jax.ShapeDtypeStruct((nhp * P_KEYS, D_MODEL), BF16),
        compiler_params=pltpu.CompilerParams(dimension_semantics=("arbitrary",)),
        name="fold_keys",
    )(keys3, wq)


def _top16(vals, idx):
    out_v, out_i = [], []
    for _ in range(P_TOPK):
        m = jnp.max(vals, axis=0, keepdims=True)
        i = jnp.min(jnp.where(vals == m, idx, float(2 ** 20)), axis=0, keepdims=True)
        out_v.append(m)
        out_i.append(i)
        vals = jnp.where(idx == i, -jnp.inf, vals)
    return jnp.concatenate(out_v, axis=0), jnp.concatenate(out_i, axis=0)


def _cand_index(tt):
    r = lax.broadcasted_iota(I32, (72, tt), 0)
    c = jnp.where(r < 24, r, jnp.where(r < 32, r + 8, r + 16))
    c = jnp.where(r >= 40, (r - 40) * P_TOPK, c)
    c = jnp.where(r >= 56, (r - 56) * P_TOPK + 1, c)
    c = jnp.where(r >= 64, (r - 64) * P_TOPK + 2, c)
    pos = jnp.where(r >= 64, r - 64, jnp.where(r >= 56, r - 56, jnp.where(r >= 40, r - 40, 4)))
    return c.astype(F32), pos < 4


def _pick16(table, sel):
    out = jnp.zeros_like(table)
    for i in range(P_TOPK):
        out = jnp.where(sel == float(i), table[i:i + 1, :], out)
    return out


def _peer_topk_kernel(ws_ref, h_ref, e1_ref, e2_ref, gt_ref, s_ref):
    s_ref[...] = lax.dot_general(ws_ref[...], h_ref[...], (((1,), (1,)), ((), ())),
                                 preferred_element_type=F32)

    tt = s_ref.shape[1]
    key_idx = lax.broadcasted_iota(I32, (P_KEYS, tt), 0).astype(F32)
    cand_idx, cand_dup = _cand_index(tt)

    def head_body(h, carry):
        r0 = pl.multiple_of(h * 2 * P_KEYS, 2 * P_KEYS)
        sv1, si1 = _top16(s_ref[pl.ds(r0, P_KEYS), :], key_idx)
        sv2, si2 = _top16(s_ref[pl.ds(r0 + P_KEYS, P_KEYS), :], key_idx)
        cand = jnp.concatenate(
            [sv1[0:1, :] + sv2]
            + [sv1[i:i + 1, :] + sv2[0:8, :] for i in (1, 2, 3)]
            + [sv1 + sv2[0:1, :]]
            + [sv1[0:8, :] + sv2[j:j + 1, :] for j in (1, 2)], axis=0)
        cv, cidx = _top16(jnp.where(cand_dup, -jnp.inf, cand), cand_idx)
        qi = jnp.floor(cidx * (1.0 / P_TOPK))
        ri = cidx - qi * P_TOPK
        e1 = _pick16(si1, qi)
        e2 = _pick16(si2, ri)
        ex = jnp.exp(cv - cv[0:1, :])
        gates = ex / jnp.sum(ex, axis=0, keepdims=True)
        o0 = pl.multiple_of(h * P_TOPK, P_TOPK)
        e1_ref[pl.ds(o0, P_TOPK), :] = e1.astype(I32)
        e2_ref[pl.ds(o0, P_TOPK), :] = e2.astype(I32)
        gt_ref[pl.ds(o0, P_TOPK), :] = gates
        return carry

    lax.fori_loop(0, P_HEADS, head_body, 0)


def peer_topk(ws, h1b, tt=256):
    n = h1b.shape[0]
    nsel = P_HEADS * P_TOPK
    nrow = P_HEADS * 2 * P_KEYS
    out = jax.ShapeDtypeStruct((nsel, n), I32)
    spec = pl.BlockSpec((nsel, tt), lambda i: (0, i))
    return pl.pallas_call(
        _peer_topk_kernel,
        grid=(n // tt,),
        in_specs=[
            pl.BlockSpec((nrow, D_MODEL), lambda i: (0, 0)),
            pl.BlockSpec((tt, D_MODEL), lambda i: (i, 0)),
        ],
        out_specs=[spec, spec, spec],
        out_shape=[out, out, jax.ShapeDtypeStruct((nsel, n), F32)],
        scratch_shapes=[pltpu.VMEM((nrow, tt), F32)],
        compiler_params=pltpu.CompilerParams(
            dimension_semantics=("arbitrary",), vmem_limit_bytes=VMEM_LIMIT),
        name="peer_topk",
    )(ws, h1b)


def _gelu(x):
    return 0.5 * x * (1.0 + lax.erf(x * (1.0 / math.sqrt(2.0))))


def _peer_ffn_kernel(h1_ref, h1b_ref, e1_ref, e2_ref, gt_ref, ulo_ref, uhi_ref, vlo_ref, vhi_ref,
                     g_ref, b_ref, out_ref, gs_ref, acc_ref, *, tt, ngrp):
    j = pl.program_id(1)
    nsel = P_HEADS * P_TOPK
    half = P_KEYS // 2

    @pl.when(j == 0)
    def _():
        acc_ref[...] = jnp.zeros_like(acc_ref)
        sub = lax.broadcasted_iota(I32, (P_KEYS, nsel), 0)
        zero = jnp.zeros((P_KEYS, nsel), F32)

        def pair_body(m, carry):
            n0 = 2 * m
            onehots = []
            gated = []
            for q in range(2):
                e1 = e1_ref[pl.ds(n0 + q, 1), :]
                e2 = e2_ref[pl.ds(n0 + q, 1), :]
                gt = gt_ref[pl.ds(n0 + q, 1), :]
                gated.append(jnp.where(sub == e1, gt, 0.0))
                onehots.append(jnp.where(sub == e2, 1.0, 0.0))
            a_cat = jnp.concatenate(gated, axis=1).astype(BF16)
            b_bd = jnp.concatenate(
                [jnp.concatenate([onehots[0], zero], axis=1),
                 jnp.concatenate([zero, onehots[1]], axis=1)], axis=0).astype(BF16)
            gn = lax.dot_general(a_cat, b_bd, (((1,), (1,)), ((), ())),
                                 preferred_element_type=F32)
            bits = pltpu.bitcast(gn, I32) + 0x8000
            packed = (bits[half:, :] & -65536) | lax.shift_right_logical(bits[:half, :], 16)
            for q in range(2):
                r0 = pl.multiple_of((n0 + q) * G_PITCH, 8)
                gs_ref[pl.ds(r0, half), :] = packed[:, q * P_KEYS:(q + 1) * P_KEYS]
            return carry

        lax.fori_loop(0, tt // 2, pair_body, 0, unroll=16)

    x = h1b_ref[...]
    sub = 2 * P_KEYS
    h_lo, h_hi = [], []
    for i in range(ngrp // 2):
        words = [gs_ref[pl.ds(j * ngrp + 2 * i + q, tt, stride=G_PITCH), :] for q in range(2)]
        g_lo = jnp.concatenate([pltpu.bitcast(w << 16, F32) for w in words], axis=1)
        g_hi = jnp.concatenate([pltpu.bitcast(w & -65536, F32) for w in words], axis=1)
        cols = slice(i * sub, (i + 1) * sub)
        s_lo = jnp.dot(x, ulo_ref[:, cols], preferred_element_type=F32)
        s_hi = jnp.dot(x, uhi_ref[:, cols], preferred_element_type=F32)
        h_lo.append((g_lo * _gelu(s_lo)).astype(BF16))
        h_hi.append((g_hi * _gelu(s_hi)).astype(BF16))
    acc = acc_ref[...]
    acc = acc + jnp.dot(jnp.concatenate(h_lo, axis=1), vlo_ref[...], preferred_element_type=F32)
    acc = acc + jnp.dot(jnp.concatenate(h_hi, axis=1), vhi_ref[...], preferred_element_type=F32)
    acc_ref[...] = acc

    @pl.when(j == pl.num_programs(1) - 1)
    def _():
        out_ref[...] = _layer_norm(ALPHA * h1_ref[...] + acc_ref[...], g_ref[...], b_ref[...])


def peer_ffn(h1, h1b, e1t, e2t, gtt, ubt, vb, g, b, tt=512, ngrp=8):
    n = h1.shape[0]
    nsel = P_HEADS * P_TOPK
    nstep = P_KEYS // 2 // ngrp
    rows = ngrp * P_KEYS
    tok = lambda i, j: (i, 0)
    full = lambda i, j: (0, 0)
    lo = lambda i, j: (j, 0)
    hi = lambda i, j: (j + nstep, 0)
    once = pl.Buffered(1)
    return pl.pallas_call(
        functools.partial(_peer_ffn_kernel, tt=tt, ngrp=ngrp),
        grid=(n // tt, nstep),
        in_specs=[
            pl.BlockSpec((tt, D_MODEL), tok, pipeline_mode=once),
            pl.BlockSpec((tt, D_MODEL), tok, pipeline_mode=once),
            pl.BlockSpec((tt, nsel), tok, pipeline_mode=once),
            pl.BlockSpec((tt, nsel), tok, pipeline_mode=once),
            pl.BlockSpec((tt, nsel), tok, pipeline_mode=once),
            pl.BlockSpec((D_MODEL, rows), lambda i, j: (0, j)),
            pl.BlockSpec((D_MODEL, rows), lambda i, j: (0, j + nstep)),
            pl.BlockSpec((rows, D_MODEL), lo),
            pl.BlockSpec((rows, D_MODEL), hi),
            pl.BlockSpec((1, D_MODEL), full),
            pl.BlockSpec((1, D_MODEL), full),
        ],
        out_specs=pl.BlockSpec((tt, D_MODEL), tok),
        out_shape=jax.ShapeDtypeStruct((n, D_MODEL), F32),
        scratch_shapes=[
            pltpu.VMEM((tt * G_PITCH, P_KEYS), I32),
            pltpu.VMEM((tt, D_MODEL), F32),
        ],
        compiler_params=pltpu.CompilerParams(
            dimension_semantics=("arbitrary", "arbitrary"), vmem_limit_bytes=VMEM_LIMIT),
        name="peer_ffn",
    )(h1, h1b, e1t, e2t, gtt, ubt, ubt, vb, vb, g, b)


def _pad_cols(w, width):
    return jnp.pad(w, ((0, 0), (0, width - w.shape[1])))


def _pad_rows(w, height):
    return jnp.pad(w, ((0, height - w.shape[0]), (0, 0)))


def _layout_a(w):
    w3 = 3 * A_WIDTH
    return jnp.concatenate([
        w[:, 0:w3], _pad_cols(w[:, w3:w3 + 64], LANES), _pad_cols(w[:, w3 + 64:w3 + 128], LANES),
        w[:, w3 + 128:A_IN]], axis=1)


def _layout_b(w):
    return jnp.concatenate([
        w[:, 0:B_QKV], _pad_cols(w[:, B_QKV:B_QKV + B_GATE_LORA], LANES),
        w[:, B_QKV + B_GATE_LORA:]], axis=1)


def kernel(x, ln_in_g, ln_in_b, w_in, a_mu, a_w0, a_w2, a_a0, a_a2, a_g2, a_k_k, a_k_a, a_r_k,
           a_gn_g, a_gn_b, b_conv, b_gk_w2, b_gk_b, b_norm_g, w_out, ln1_g, ln1_b, p_wq, p_keys,
           p_u, p_v, ln2_g, ln2_b):
    bsz, t, d = x.shape
    n = bsz * t
    row = lambda a: a.reshape(1, -1)

    h = x.reshape(n, d)
    for l in range(DEPTH):
        wa = _layout_a(w_in[l][:, :A_IN]).astype(BF16)
        wb = _layout_b(w_in[l][:, A_IN:]).astype(BF16)
        if l == 0:
            h, pa, pb = ln_proj(h, row(ln_in_g), row(ln_in_b), wa, wb)
        ya = rwkv7(
            pa.reshape(bsz, t, A_PAD), _layout_a(row(a_mu[l])), row(a_w0[l]),
            _pad_rows(a_w2[l], LANES), row(a_a0[l]), _pad_rows(a_a2[l], LANES), a_g2[l],
            row(a_k_k[l]), row(a_k_a[l]), row(a_r_k[l]), row(a_gn_g[l]), row(a_gn_b[l]))
        yb = gla(pb.reshape(bsz, t, B_PAD), b_conv[l], _pad_rows(b_gk_w2[l], LANES),
                 row(b_gk_b[l]), row(b_norm_g[l]))
        h1, h1b = out_ln(ya.reshape(n, A_WIDTH), yb.reshape(n, B_WIDTH), h,
                         w_out[l][:A_WIDTH].astype(BF16), w_out[l][A_WIDTH:].astype(BF16),
                         row(ln1_g[l]), row(ln1_b[l]))
        ws = fold_keys(p_keys[l].reshape(P_HEADS * 2, P_KEYS, P_HALF), p_wq[l])
        e1, e2, gt = peer_topk(ws, h1b)
        h = peer_ffn(h1, h1b, e1.T, e2.T, gt.T, p_u[l].astype(BF16).T, p_v[l].astype(BF16),
                     row(ln2_g[l]), row(ln2_b[l]))
    return h.reshape(bsz, t, d)
```

```python
import functools
import math

import jax
import jax.numpy as jnp
from jax import lax
from jax.experimental import pallas as pl
from jax.experimental.pallas import tpu as pltpu

F32 = jnp.float32
BF16 = jnp.bfloat16
I32 = jnp.int32

D_MODEL = 1024
DEPTH = 1
CHUNK = 64
LN_EPS = 1e-5
ALPHA = (2.0 * DEPTH) ** 0.25

A_HEAD = 64
A_WIDTH = 512
A_HEADS = 8
A_GN_EPS = 64e-5
A_IN = 1792
A_PAD = 1920

B_HEADS = 4
B_DV = 128
B_DK = 64
B_WIDTH = 512
B_WIDTH_K = 256
B_QKV = 1024
B_GATE_LORA = 16
B_GATE_NORM = 16.0
B_CONV = 4
B_RMS_EPS = 1e-5
B_PAD = 1664

P_HEADS = 8
P_KEYS = 128
P_HALF = 128
P_TOPK = 16
P_EXPERTS = P_KEYS * P_KEYS

LANES = 128
G_PITCH = 72

VMEM_LIMIT = 56 * 1024 * 1024


def _dot(a, b):
    return jnp.dot(a.astype(BF16), b.astype(BF16), preferred_element_type=F32)


def _dot_nt(a, b):
    return lax.dot_general(a.astype(BF16), b.astype(BF16), (((1,), (1,)), ((), ())),
                           preferred_element_type=F32)


def _dot_tn(a, b):
    return lax.dot_general(a.astype(BF16), b.astype(BF16), (((0,), (0,)), ((), ())),
                           preferred_element_type=F32)


def _split3(x):
    hi = x.astype(BF16)
    r1 = x - hi.astype(F32)
    mid = r1.astype(BF16)
    lo = (r1 - mid.astype(F32)).astype(BF16)
    return hi, mid, lo


def _dot_exact_rhs(a01, x):
    a = a01.astype(BF16)
    hi, mid, lo = _split3(x)
    out = jnp.dot(a, hi, preferred_element_type=F32)
    out += jnp.dot(a, mid, preferred_element_type=F32)
    out += jnp.dot(a, lo, preferred_element_type=F32)
    return out


def _dot_exact_lhs(x, b01):
    b = b01.astype(BF16)
    hi, mid, lo = _split3(x)
    out = jnp.dot(hi, b, preferred_element_type=F32)
    out += jnp.dot(mid, b, preferred_element_type=F32)
    out += jnp.dot(lo, b, preferred_element_type=F32)
    return out


def _sigmoid(x):
    return 1.0 / (1.0 + jnp.exp(-x))


def _softplus(x):
    return jnp.maximum(x, 0.0) + jnp.log(1.0 + jnp.exp(-jnp.abs(x)))


def _silu(x):
    return x * _sigmoid(x)


def _layer_norm(x, g, b):
    mu = jnp.mean(x, axis=-1, keepdims=True)
    xc = x - mu
    var = jnp.mean(xc * xc, axis=-1, keepdims=True)
    return xc * lax.rsqrt(var + LN_EPS) * g + b


def _block_diag_ones(n, seg):
    r = lax.broadcasted_iota(I32, (n, n), 0) // seg
    c = lax.broadcasted_iota(I32, (n, n), 1) // seg
    return jnp.where(r == c, 1.0, 0.0).astype(F32)


def _chunk_tril(n, seg):
    r = lax.broadcasted_iota(I32, (n, n), 0)
    c = lax.broadcasted_iota(I32, (n, n), 1)
    return jnp.where((r // seg == c // seg) & (c <= r), 1.0, 0.0).astype(F32)


def _ln_proj_kernel(x_ref, g_ref, b_ref, wa_ref, wb_ref, h_ref, pa_ref, pb_ref):
    h = _layer_norm(x_ref[...], g_ref[...], b_ref[...])
    h_ref[...] = h
    hb = h.astype(BF16)
    pa_ref[...] = jnp.dot(hb, wa_ref[...], preferred_element_type=F32)
    pb_ref[...] = jnp.dot(hb, wb_ref[...], preferred_element_type=F32)


def ln_proj(x2, g, b, wa, wb, tm=512):
    n = x2.shape[0]
    full = lambda i: (0, 0)
    return pl.pallas_call(
        _ln_proj_kernel,
        grid=(n // tm,),
        in_specs=[
            pl.BlockSpec((tm, D_MODEL), lambda i: (i, 0)),
            pl.BlockSpec((1, D_MODEL), full),
            pl.BlockSpec((1, D_MODEL), full),
            pl.BlockSpec((D_MODEL, A_PAD), full),
            pl.BlockSpec((D_MODEL, B_PAD), full),
        ],
        out_specs=[
            pl.BlockSpec((tm, D_MODEL), lambda i: (i, 0)),
            pl.BlockSpec((tm, A_PAD), lambda i: (i, 0)),
            pl.BlockSpec((tm, B_PAD), lambda i: (i, 0)),
        ],
        out_shape=[
            jax.ShapeDtypeStruct((n, D_MODEL), F32),
            jax.ShapeDtypeStruct((n, A_PAD), F32),
            jax.ShapeDtypeStruct((n, B_PAD), F32),
        ],
        compiler_params=pltpu.CompilerParams(
            dimension_semantics=("arbitrary",), vmem_limit_bytes=VMEM_LIMIT),
        name="ln_proj",
    )(x2, g, b, wa, wb)


def _rwkv_kernel(pa_ref, mu_ref, w0_ref, w2_ref, a0_ref, a2_ref, g2_ref, kkw_ref, kaw_ref,
                 rk_ref, gng_ref, gnb_ref, ya_ref,
                 carry_ref, z_ref, at_s, rt_s, bt_s, kt_s, bh_s, kh_s, v_s, el_s, y_s, *, tb):
    nck = tb // CHUNK
    w = A_WIDTH

    @pl.when(pl.program_id(1) == 0)
    def _():
        carry_ref[...] = jnp.zeros_like(carry_ref)
        z_ref[...] = jnp.zeros_like(z_ref)

    p = pa_ref[0]
    rolled = pltpu.roll(p, 1, axis=0)
    row = lax.broadcasted_iota(I32, p.shape, 0)
    prev = jnp.where(row == 0, carry_ref[0:1, :], rolled)
    carry_ref[0:1, :] = p[tb - 1:tb, :]
    ps = p + (prev - p) * mu_ref[...]

    r = ps[:, 0:w]
    k = ps[:, w:2 * w]
    v = ps[:, 2 * w:3 * w]
    wd = ps[:, 3 * w:3 * w + LANES]
    ad = ps[:, 3 * w + LANES:3 * w + 2 * LANES]
    gd = ps[:, 3 * w + 2 * LANES:3 * w + 3 * LANES]

    wlin = w0_ref[...] + _dot(jnp.tanh(wd), w2_ref[...])
    lw = -jnp.exp(-_softplus(-wlin) - 0.5)
    iclr = _sigmoid(a0_ref[...] + _dot(ad, a2_ref[...]))
    gate = _dot(_sigmoid(gd), g2_ref[...])

    bd = _block_diag_ones(w, A_HEAD)
    kkr = k * kkw_ref[...]
    ss = _dot_exact_lhs(kkr * kkr, bd)
    kk = kkr / jnp.maximum(jnp.sqrt(ss), 1e-12)
    kh = k * (1.0 + (iclr - 1.0) * kaw_ref[...])
    b = kk * iclr
    bonus = _dot_exact_lhs(r * kh * rk_ref[...], bd) * v

    lcum = _dot_exact_rhs(_chunk_tril(tb, CHUNK), lw)
    e_l = jnp.exp(lcum)
    e_nl = jnp.exp(-lcum)
    at_s[...] = -kk * jnp.exp(lcum - lw)
    rt_s[...] = r * e_l
    bt_s[...] = b * e_nl
    kt_s[...] = kh * e_nl
    v_s[...] = v
    for c in range(nck):
        rows = slice(c * CHUNK, (c + 1) * CHUNK)
        l_end = lcum[(c + 1) * CHUNK - 1:(c + 1) * CHUNK, :]
        e_lc = jnp.exp(l_end - lcum[rows, :])
        bh_s[rows, :] = b[rows, :] * e_lc
        kh_s[rows, :] = kh[rows, :] * e_lc
        el_s[c:c + 1, :] = jnp.exp(l_end)

    ri = lax.broadcasted_iota(I32, (2 * CHUNK, 2 * CHUNK), 0)
    ci = lax.broadcasted_iota(I32, (2 * CHUNK, 2 * CHUNK), 1) % CHUNK
    tri_mask = ci < jnp.where(ri < CHUNK, ri, ri - (CHUNK - 1))
    er = lax.broadcasted_iota(I32, (CHUNK, CHUNK), 0)
    ec = lax.broadcasted_iota(I32, (CHUNK, CHUNK), 1)
    eye_mask = er == ec
    eye = jnp.where(eye_mask, 1.0, 0.0).astype(F32)

    heads = range(A_HEADS)
    chains = [(c, h) for c in range(nck) for h in heads]
    rows = [slice(c * CHUNK, (c + 1) * CHUNK) for c in range(nck)]
    slabs = [{name: ref[rows[c], :] for name, ref in
              (("at", at_s), ("rt", rt_s), ("bt", bt_s), ("kt", kt_s), ("bh", bh_s), ("kh", kh_s), ("v", v_s))}
             for c in range(nck)]
    els = [el_s[c:c + 1, :] for c in range(nck)]

    def part(name, c, h):
        return slabs[c][name][:, h * A_HEAD:(h + 1) * A_HEAD]

    gms = [jnp.where(tri_mask,
                     _dot_nt(jnp.concatenate([part("at", c, h), part("rt", c, h)], axis=0),
                             jnp.concatenate([part("bt", c, h), part("kt", c, h)], axis=0)), 0.0)
           for c, h in chains]
    mvs = [_dot(gm[:, CHUNK:2 * CHUNK], part("v", c, h)) for gm, (c, h) in zip(gms, chains)]
    khv = [_dot_tn(part("kh", c, h), part("v", c, h)) for c, h in chains]
    pws = [gm[0:CHUNK, 0:CHUNK] for gm in gms]
    tms = [eye + pw for pw in pws]
    for _ in range(5):
        pws = [_dot(pw, pw) for pw in pws]
        tms = [tm + _dot(tm, pw) for tm, pw in zip(tms, pws)]
    p12s = [_dot(tm, jnp.concatenate([mv[0:CHUNK], part("at", c, h)], axis=1))
            for tm, mv, (c, h) in zip(tms, mvs, chains)]
    yqs = [jnp.concatenate([mv[CHUNK:2 * CHUNK], part("rt", c, h)], axis=1)
           + _dot(gm[CHUNK:2 * CHUNK, 0:CHUNK], p12)
           for mv, gm, p12, (c, h) in zip(mvs, gms, p12s, chains)]
    zws = [jnp.concatenate(
               [kv, jnp.where(eye_mask,
                              jnp.broadcast_to(els[c][:, h * A_HEAD:(h + 1) * A_HEAD], (CHUNK, CHUNK)), 0.0)],
               axis=1)
           + _dot_tn(part("bh", c, h), p12)
           for kv, p12, (c, h) in zip(khv, p12s, chains)]

    zs = [z_ref[h] for h in heads]
    for c in range(nck):
        ys = []
        for h in heads:
            yq, zw = yqs[c * A_HEADS + h], zws[c * A_HEADS + h]
            ys.append(yq[:, 0:CHUNK] + _dot(yq[:, CHUNK:2 * CHUNK], zs[h]))
            zs[h] = zw[:, 0:CHUNK] + _dot(zw[:, CHUNK:2 * CHUNK], zs[h])
        y_s[rows[c], :] = jnp.concatenate(ys, axis=1)
    for h in heads:
        z_ref[h] = zs[h]

    y = y_s[...]
    inv_n = 1.0 / A_HEAD
    mean = _dot_exact_lhs(y, bd) * inv_n
    yc = y - mean
    var = _dot_exact_lhs(yc * yc, bd) * inv_n
    yn = yc * lax.rsqrt(var + A_GN_EPS) * gng_ref[...] + gnb_ref[...]
    ya_ref[0] = (yn + bonus) * gate


def rwkv7(pa3, mu, w0, w2, a0, a2, g2, kkw, kaw, rk, gng, gnb, tb=256):
    bsz, t, _ = pa3.shape
    full = lambda bi, ti: (0, 0)
    vec = pl.BlockSpec((1, A_WIDTH), full)
    slab = pltpu.VMEM((tb, A_WIDTH), F32)
    return pl.pallas_call(
        functools.partial(_rwkv_kernel, tb=tb),
        grid=(bsz, t // tb),
        in_specs=[
            pl.BlockSpec((1, tb, A_PAD), lambda bi, ti: (bi, ti, 0)),
            pl.BlockSpec((1, A_PAD), full),
            vec,
            pl.BlockSpec((LANES, A_WIDTH), full),
            vec,
            pl.BlockSpec((LANES, A_WIDTH), full),
            pl.BlockSpec((LANES, A_WIDTH), full),
            vec, vec, vec, vec, vec,
        ],
        out_specs=pl.BlockSpec((1, tb, A_WIDTH), lambda bi, ti: (bi, ti, 0)),
        out_shape=jax.ShapeDtypeStruct((bsz, t, A_WIDTH), F32),
        scratch_shapes=[
            pltpu.VMEM((8, A_PAD), F32),
            pltpu.VMEM((A_HEADS, A_HEAD, A_HEAD), F32),
            slab, slab, slab, slab, slab, slab, slab,
            pltpu.VMEM((8, A_WIDTH), F32),
            slab,
        ],
        compiler_params=pltpu.CompilerParams(
            dimension_semantics=("arbitrary", "arbitrary"), vmem_limit_bytes=VMEM_LIMIT),
        name="rwkv7",
    )(pa3, mu, w0, w2, a0, a2, g2, kkw, kaw, rk, gng, gnb)


def _gla_kernel(pb_ref, cw_ref, gw_ref, gb_ref, ng_ref, yb_ref,
                xpad_ref, s_ref, qt_s, kt_s, kh_s, v_s, gk_s, o_s, *, tb):
    nck = tb // CHUNK

    @pl.when(pl.program_id(1) == 0)
    def _():
        xpad_ref[0:8, :] = jnp.zeros((8, B_QKV), F32)
        s_ref[...] = jnp.zeros_like(s_ref)

    p = pb_ref[0]
    xpad_ref[8:8 + tb, :] = p[:, 0:B_QKV]
    conv = cw_ref[B_CONV - 1:B_CONV, :] * p[:, 0:B_QKV]
    for j in range(B_CONV - 1):
        sh = B_CONV - 1 - j
        conv = conv + cw_ref[j:j + 1, :] * xpad_ref[8 - sh:8 - sh + tb, :]
    xpad_ref[0:8, :] = p[tb - 8:tb, 0:B_QKV]
    qkv = _silu(conv)
    q = qkv[:, 0:B_WIDTH_K] * (B_DK ** -0.5)
    k = qkv[:, B_WIDTH_K:2 * B_WIDTH_K]
    v_s[...] = qkv[:, 2 * B_WIDTH_K:]
    gkd = p[:, B_QKV:B_QKV + LANES]
    gz = _dot(gkd, gw_ref[...]) + gb_ref[...]
    gk = -_softplus(-gz) * (1.0 / B_GATE_NORM)
    gk_s[...] = gk
    gcum = _dot_exact_rhs(_chunk_tril(tb, CHUNK), gk)
    qt_s[...] = q * jnp.exp(gcum)
    kt_s[...] = k * jnp.exp(-gcum)
    for c in range(nck):
        rows = slice(c * CHUNK, (c + 1) * CHUNK)
        g_end = gcum[(c + 1) * CHUNK - 1:(c + 1) * CHUNK, :]
        kh_s[rows, :] = k[rows, :] * jnp.exp(g_end - gcum[rows, :])

    ri = lax.broadcasted_iota(I32, (CHUNK, CHUNK), 0)
    ci = lax.broadcasted_iota(I32, (CHUNK, CHUNK), 1)
    causal = ci <= ri
    ones_cv = jnp.ones((CHUNK, B_DV), F32)

    def chunk_body(c, carry):
        r0 = pl.multiple_of(c * CHUNK, CHUNK)
        qt_c = qt_s[pl.ds(r0, CHUNK), :]
        kt_c = kt_s[pl.ds(r0, CHUNK), :]
        kh_c = kh_s[pl.ds(r0, CHUNK), :]
        v_c = v_s[pl.ds(r0, CHUNK), :]
        gk_c = gk_s[pl.ds(r0, CHUNK), :]
        os_ = []
        for h in range(B_HEADS):
            sk = slice(h * B_DK, (h + 1) * B_DK)
            sv = slice(h * B_DV, (h + 1) * B_DV)
            a = jnp.where(causal, _dot_nt(qt_c[:, sk], kt_c[:, sk]), 0.0)
            s = s_ref[h]
            os_.append(_dot(a, v_c[:, sv]) + _dot(qt_c[:, sk], s))
            hi, mid, lo = _split3(gk_c[:, sk])
            ob = ones_cv.astype(BF16)
            dn = (((0,), (0,)), ((), ()))
            gcol = (lax.dot_general(hi, ob, dn, preferred_element_type=F32)
                    + lax.dot_general(mid, ob, dn, preferred_element_type=F32)
                    + lax.dot_general(lo, ob, dn, preferred_element_type=F32))
            s_ref[h] = s * jnp.exp(gcol) + _dot_tn(kh_c[:, sk], v_c[:, sv])
        o_s[pl.ds(r0, CHUNK), :] = jnp.concatenate(os_, axis=1)
        return carry

    lax.fori_loop(0, nck, chunk_body, 0)

    g = p[:, B_QKV + LANES:]
    outs = []
    for h in range(B_HEADS):
        sv = slice(h * B_DV, (h + 1) * B_DV)
        o = o_s[:, sv]
        o = o * lax.rsqrt(jnp.mean(o * o, axis=-1, keepdims=True) + B_RMS_EPS) * ng_ref[...]
        outs.append(o * _silu(g[:, sv]))
    yb_ref[0] = jnp.concatenate(outs, axis=1)


def gla(pb3, cw, gw, gb, ng, tb=256):
    bsz, t, _ = pb3.shape
    full = lambda bi, ti: (0, 0)
    return pl.pallas_call(
        functools.partial(_gla_kernel, tb=tb),
        grid=(bsz, t // tb),
        in_specs=[
            pl.BlockSpec((1, tb, B_PAD), lambda bi, ti: (bi, ti, 0)),
            pl.BlockSpec((B_CONV, B_QKV), full),
            pl.BlockSpec((LANES, B_WIDTH_K), full),
            pl.BlockSpec((1, B_WIDTH_K), full),
            pl.BlockSpec((1, B_DV), full),
        ],
        out_specs=pl.BlockSpec((1, tb, B_WIDTH), lambda bi, ti: (bi, ti, 0)),
        out_shape=jax.ShapeDtypeStruct((bsz, t, B_WIDTH), F32),
        scratch_shapes=[
            pltpu.VMEM((tb + 8, B_QKV), F32),
            pltpu.VMEM((B_HEADS, B_DK, B_DV), F32),
            pltpu.VMEM((tb, B_WIDTH_K), F32),
            pltpu.VMEM((tb, B_WIDTH_K), F32),
            pltpu.VMEM((tb, B_WIDTH_K), F32),
            pltpu.VMEM((tb, B_WIDTH), F32),
            pltpu.VMEM((tb, B_WIDTH_K), F32),
            pltpu.VMEM((tb, B_WIDTH), F32),
        ],
        compiler_params=pltpu.CompilerParams(
            dimension_semantics=("arbitrary", "arbitrary"), vmem_limit_bytes=VMEM_LIMIT),
        name="gla",
    )(pb3, cw, gw, gb, ng)


def _out_ln_kernel(ya_ref, yb_ref, h_ref, wa_ref, wb_ref, g_ref, b_ref, h1_ref, h1b_ref):
    y = _dot(ya_ref[...], wa_ref[...]) + _dot(yb_ref[...], wb_ref[...])
    h1 = _layer_norm(ALPHA * h_ref[...] + y, g_ref[...], b_ref[...])
    h1_ref[...] = h1
    h1b_ref[...] = h1.astype(BF16)


def out_ln(ya, yb, h, wa, wb, g, b, tm=512):
    n = h.shape[0]
    full = lambda i: (0, 0)
    return pl.pallas_call(
        _out_ln_kernel,
        grid=(n // tm,),
        in_specs=[
            pl.BlockSpec((tm, A_WIDTH), lambda i: (i, 0)),
            pl.BlockSpec((tm, B_WIDTH), lambda i: (i, 0)),
            pl.BlockSpec((tm, D_MODEL), lambda i: (i, 0)),
            pl.BlockSpec((A_WIDTH, D_MODEL), full),
            pl.BlockSpec((B_WIDTH, D_MODEL), full),
            pl.BlockSpec((1, D_MODEL), full),
            pl.BlockSpec((1, D_MODEL), full),
        ],
        out_specs=[
            pl.BlockSpec((tm, D_MODEL), lambda i: (i, 0)),
            pl.BlockSpec((tm, D_MODEL), lambda i: (i, 0)),
        ],
        out_shape=[
            jax.ShapeDtypeStruct((n, D_MODEL), F32),
            jax.ShapeDtypeStruct((n, D_MODEL), BF16),
        ],
        compiler_params=pltpu.CompilerParams(
            dimension_semantics=("arbitrary",), vmem_limit_bytes=VMEM_LIMIT),
        name="out_ln",
    )(ya, yb, h, wa, wb, g, b)


def _fold_keys_kernel(keys_ref, wq_ref, out_ref):
    kh, km, kl = _split3(keys_ref[0])
    wh, wm, wl = _split3(wq_ref[...])
    dn = (((1,), (1,)), ((), ()))
    acc = lax.dot_general(kh, wh, dn, preferred_element_type=F32)
    for a, b in ((kh, wm), (km, wh), (km, wm), (kh, wl), (kl, wh)):
        acc += lax.dot_general(a, b, dn, preferred_element_type=F32)
    out_ref[...] = acc.astype(BF16)


def fold_keys(keys3, wq):
    nhp = keys3.shape[0]
    return pl.pallas_call(
        _fold_keys_kernel,
        grid=(nhp,),
        in_specs=[
            pl.BlockSpec((1, P_KEYS, P_HALF), lambda i: (i, 0, 0)),
            pl.BlockSpec((D_MODEL, P_HALF), lambda i: (0, i)),
        ],
        out_specs=pl.BlockSpec((P_KEYS, D_MODEL), lambda i: (i, 0)),
        out_shape=jax.ShapeDtypeStruct((nhp * P_KEYS, D_MODEL), BF16),
        compiler_params=pltpu.CompilerParams(dimension_semantics=("arbitrary",)),
        name="fold_keys",
    )(keys3, wq)


def _top16(vals, idx):
    out_v, out_i = [], []
    for _ in range(P_TOPK):
        m = jnp.max(vals, axis=0, keepdims=True)
        i = jnp.min(jnp.where(vals == m, idx, float(2 ** 20)), axis=0, keepdims=True)
        out_v.append(m)
        out_i.append(i)
        vals = jnp.where(idx == i, -jnp.inf, vals)
    return jnp.concatenate(out_v, axis=0), jnp.concatenate(out_i, axis=0)


def _cand_index(tt):
    r = lax.broadcasted_iota(I32, (72, tt), 0)
    c = jnp.where(r < 24, r, jnp.where(r < 32, r + 8, r + 16))
    c = jnp.where(r >= 40, (r - 40) * P_TOPK, c)
    c = jnp.where(r >= 56, (r - 56) * P_TOPK + 1, c)
    c = jnp.where(r >= 64, (r - 64) * P_TOPK + 2, c)
    pos = jnp.where(r >= 64, r - 64, jnp.where(r >= 56, r - 56, jnp.where(r >= 40, r - 40, 4)))
    return c.astype(F32), pos < 4


def _pick16(table, sel):
    out = jnp.zeros_like(table)
    for i in range(P_TOPK):
        out = jnp.where(sel == float(i), table[i:i + 1, :], out)
    return out


def _peer_topk_kernel(ws_ref, h_ref, e1_ref, e2_ref, gt_ref, s_ref):
    s_ref[...] = lax.dot_general(ws_ref[...], h_ref[...], (((1,), (1,)), ((), ())),
                                 preferred_element_type=F32)

    tt = s_ref.shape[1]
    key_idx = lax.broadcasted_iota(I32, (P_KEYS, tt), 0).astype(F32)
    cand_idx, cand_dup = _cand_index(tt)

    def head_body(h, carry):
        r0 = pl.multiple_of(h * 2 * P_KEYS, 2 * P_KEYS)
        sv1, si1 = _top16(s_ref[pl.ds(r0, P_KEYS), :], key_idx)
        sv2, si2 = _top16(s_ref[pl.ds(r0 + P_KEYS, P_KEYS), :], key_idx)
        cand = jnp.concatenate(
            [sv1[0:1, :] + sv2]
            + [sv1[i:i + 1, :] + sv2[0:8, :] for i in (1, 2, 3)]
            + [sv1 + sv2[0:1, :]]
            + [sv1[0:8, :] + sv2[j:j + 1, :] for j in (1, 2)], axis=0)
        cv, cidx = _top16(jnp.where(cand_dup, -jnp.inf, cand), cand_idx)
        qi = jnp.floor(cidx * (1.0 / P_TOPK))
        ri = cidx - qi * P_TOPK
        e1 = _pick16(si1, qi)
        e2 = _pick16(si2, ri)
        ex = jnp.exp(cv - cv[0:1, :])
        gates = ex / jnp.sum(ex, axis=0, keepdims=True)
        o0 = pl.multiple_of(h * P_TOPK, P_TOPK)
        e1_ref[pl.ds(o0, P_TOPK), :] = e1.astype(I32)
        e2_ref[pl.ds(o0, P_TOPK), :] = e2.astype(I32)
        gt_ref[pl.ds(o0, P_TOPK), :] = gates
        return carry

    lax.fori_loop(0, P_HEADS, head_body, 0)


def peer_topk(ws, h1b, tt=256):
    n = h1b.shape[0]
    nsel = P_HEADS * P_TOPK
    nrow = P_HEADS * 2 * P_KEYS
    out = jax.ShapeDtypeStruct((nsel, n), I32)
    spec = pl.BlockSpec((nsel, tt), lambda i: (0, i))
    return pl.pallas_call(
        _peer_topk_kernel,
        grid=(n // tt,),
        in_specs=[
            pl.BlockSpec((nrow, D_MODEL), lambda i: (0, 0)),
            pl.BlockSpec((tt, D_MODEL), lambda i: (i, 0)),
        ],
        out_specs=[spec, spec, spec],
        out_shape=[out, out, jax.ShapeDtypeStruct((nsel, n), F32)],
        scratch_shapes=[pltpu.VMEM((nrow, tt), F32)],
        compiler_params=pltpu.CompilerParams(
            dimension_semantics=("arbitrary",), vmem_limit_bytes=VMEM_LIMIT),
        name="peer_topk",
    )(ws, h1b)


def _gelu(x):
    return 0.5 * x * (1.0 + lax.erf(x * (1.0 / math.sqrt(2.0))))


def _peer_ffn_kernel(h1_ref, h1b_ref, e1_ref, e2_ref, gt_ref, ulo_ref, uhi_ref, vlo_ref, vhi_ref,
                     g_ref, b_ref, out_ref, gs_ref, acc_ref, *, tt, ngrp):
    j = pl.program_id(1)
    nsel = P_HEADS * P_TOPK
    half = P_KEYS // 2

    @pl.when(j == 0)
    def _():
        acc_ref[...] = jnp.zeros_like(acc_ref)
        sub = lax.broadcasted_iota(I32, (P_KEYS, nsel), 0)
        zero = jnp.zeros((P_KEYS, nsel), F32)

        def pair_body(m, carry):
            n0 = 2 * m
            onehots = []
            gated = []
            for q in range(2):
                e1 = e1_ref[pl.ds(n0 + q, 1), :]
                e2 = e2_ref[pl.ds(n0 + q, 1), :]
                gt = gt_ref[pl.ds(n0 + q, 1), :]
                gated.append(jnp.where(sub == e1, gt, 0.0))
                onehots.append(jnp.where(sub == e2, 1.0, 0.0))
            a_cat = jnp.concatenate(gated, axis=1).astype(BF16)
            b_bd = jnp.concatenate(
                [jnp.concatenate([onehots[0], zero], axis=1),
                 jnp.concatenate([zero, onehots[1]], axis=1)], axis=0).astype(BF16)
            gn = lax.dot_general(a_cat, b_bd, (((1,), (1,)), ((), ())),
                                 preferred_element_type=F32)
            bits = pltpu.bitcast(gn, I32) + 0x8000
            packed = (bits[half:, :] & -65536) | lax.shift_right_logical(bits[:half, :], 16)
            for q in range(2):
                r0 = pl.multiple_of((n0 + q) * G_PITCH, 8)
                gs_ref[pl.ds(r0, half), :] = packed[:, q * P_KEYS:(q + 1) * P_KEYS]
            return carry

        lax.fori_loop(0, tt // 2, pair_body, 0, unroll=16)

    x = h1b_ref[...]
    sub = 2 * P_KEYS
    h_lo, h_hi = [], []
    for i in range(ngrp // 2):
        words = [gs_ref[pl.ds(j * ngrp + 2 * i + q, tt, stride=G_PITCH), :] for q in range(2)]
        g_lo = jnp.concatenate([pltpu.bitcast(w << 16, F32) for w in words], axis=1)
        g_hi = jnp.concatenate([pltpu.bitcast(w & -65536, F32) for w in words], axis=1)
        cols = slice(i * sub, (i + 1) * sub)
        s_lo = jnp.dot(x, ulo_ref[:, cols], preferred_element_type=F32)
        s_hi = jnp.dot(x, uhi_ref[:, cols], preferred_element_type=F32)
        h_lo.append((g_lo * _gelu(s_lo)).astype(BF16))
        h_hi.append((g_hi * _gelu(s_hi)).astype(BF16))
    acc = acc_ref[...]
    acc = acc + jnp.dot(jnp.concatenate(h_lo, axis=1), vlo_ref[...], preferred_element_type=F32)
    acc = acc + jnp.dot(jnp.concatenate(h_hi, axis=1), vhi_ref[...], preferred_element_type=F32)
    acc_ref[...] = acc

    @pl.when(j == pl.num_programs(1) - 1)
    def _():
        out_ref[...] = _layer_norm(ALPHA * h1_ref[...] + acc_ref[...], g_ref[...], b_ref[...])


def peer_ffn(h1, h1b, e1t, e2t, gtt, ubt, vb, g, b, tt=512, ngrp=8):
    n = h1.shape[0]
    nsel = P_HEADS * P_TOPK
    nstep = P_KEYS // 2 // ngrp
    rows = ngrp * P_KEYS
    tok = lambda i, j: (i, 0)
    full = lambda i, j: (0, 0)
    lo = lambda i, j: (j, 0)
    hi = lambda i, j: (j + nstep, 0)
    once = pl.Buffered(1)
    return pl.pallas_call(
        functools.partial(_peer_ffn_kernel, tt=tt, ngrp=ngrp),
        grid=(n // tt, nstep),
        in_specs=[
            pl.BlockSpec((tt, D_MODEL), tok, pipeline_mode=once),
            pl.BlockSpec((tt, D_MODEL), tok, pipeline_mode=once),
            pl.BlockSpec((tt, nsel), tok, pipeline_mode=once),
            pl.BlockSpec((tt, nsel), tok, pipeline_mode=once),
            pl.BlockSpec((tt, nsel), tok, pipeline_mode=once),
            pl.BlockSpec((D_MODEL, rows), lambda i, j: (0, j)),
            pl.BlockSpec((D_MODEL, rows), lambda i, j: (0, j + nstep)),
            pl.BlockSpec((rows, D_MODEL), lo),
            pl.BlockSpec((rows, D_MODEL), hi),
            pl.BlockSpec((1, D_MODEL), full),
            pl.BlockSpec((1, D_MODEL), full),
        ],
        out_specs=pl.BlockSpec((tt, D_MODEL), tok),
        out_shape=jax.ShapeDtypeStruct((n, D_MODEL), F32),
        scratch_shapes=[
            pltpu.VMEM((tt * G_PITCH, P_KEYS), I32),
            pltpu.VMEM((tt, D_MODEL), F32),
        ],
        compiler_params=pltpu.CompilerParams(
            dimension_semantics=("arbitrary", "arbitrary"), vmem_limit_bytes=VMEM_LIMIT),
        name="peer_ffn",
    )(h1, h1b, e1t, e2t, gtt, ubt, ubt, vb, vb, g, b)


def _top16_steps(vals, idx):
    out_v, out_i = [], []
    for _ in range(P_TOPK):
        m = jnp.max(vals, axis=0, keepdims=True)
        i = jnp.min(jnp.where(vals == m, idx, float(2 ** 20)), axis=0, keepdims=True)
        out_v.append(m)
        out_i.append(i)
        vals = jnp.where(idx == i, -jnp.inf, vals)
        yield
    return jnp.concatenate(out_v, axis=0), jnp.concatenate(out_i, axis=0)


def _head_topk_steps(h, half, s_scr, e1s, e2s, gts, key_idx, cand_idx, cand_dup):
    r0 = pl.multiple_of(h * 2 * P_KEYS, 2 * P_KEYS)
    sv1, si1 = yield from _top16_steps(s_scr[half, pl.ds(r0, P_KEYS), :], key_idx)
    sv2, si2 = yield from _top16_steps(s_scr[half, pl.ds(r0 + P_KEYS, P_KEYS), :], key_idx)
    cand = jnp.concatenate(
        [sv1[0:1, :] + sv2]
        + [sv1[i:i + 1, :] + sv2[0:8, :] for i in (1, 2, 3)]
        + [sv1 + sv2[0:1, :]]
        + [sv1[0:8, :] + sv2[j:j + 1, :] for j in (1, 2)], axis=0)
    cv, cidx = yield from _top16_steps(jnp.where(cand_dup, -jnp.inf, cand), cand_idx)
    qi = jnp.floor(cidx * (1.0 / P_TOPK))
    ri = cidx - qi * P_TOPK
    ex = jnp.exp(cv - cv[0:1, :])
    o0 = pl.multiple_of(h * P_TOPK, P_TOPK)
    e1s[half, pl.ds(o0, P_TOPK), :] = _pick16(si1, qi)
    e2s[half, pl.ds(o0, P_TOPK), :] = _pick16(si2, ri)
    gts[half, pl.ds(o0, P_TOPK), :] = ex / jnp.sum(ex, axis=0, keepdims=True)


def _head_topk(*args, **kwargs):
    for _ in _head_topk_steps(*args, **kwargs):
        pass


def _peer_fused_kernel(h1_ref, h1b_ref, h1bn_ref, ws_ref, ulo_ref, uhi_ref, vlo_ref, vhi_ref,
                       g_ref, b_ref, out_ref,
                       gs_ref, acc_ref, s_scr, e1s, e2s, gts, tm1, tm2, tmg, *, tt, ngrp):
    i = pl.program_id(0)
    j = pl.program_id(1)
    nsel = P_HEADS * P_TOPK
    half_keys = P_KEYS // 2
    th = tt // 2
    key_idx = lax.broadcasted_iota(I32, (P_KEYS, th), 0).astype(F32)
    cand_idx, cand_dup = _cand_index(th)
    topk = functools.partial(_head_topk, s_scr=s_scr, e1s=e1s, e2s=e2s, gts=gts,
                             key_idx=key_idx, cand_idx=cand_idx, cand_dup=cand_dup)

    def scores(hb_ref):
        for half in range(2):
            s_scr[half] = lax.dot_general(ws_ref[...], hb_ref[half * th:(half + 1) * th, :],
                                          (((1,), (1,)), ((), ())), preferred_element_type=F32)

    @pl.when((i == 0) & (j == 0))
    def _():
        scores(h1b_ref)

        def body(t, carry):
            topk(t // 2, t % 2)
            return carry

        lax.fori_loop(0, 2 * P_HEADS, body, 0)

    @pl.when(j == 0)
    def _():
        acc_ref[...] = jnp.zeros_like(acc_ref)
        for half in range(2):
            rows = slice(half * th, (half + 1) * th)
            tm1[rows, :] = e1s[half].T
            tm2[rows, :] = e2s[half].T
            tmg[rows, :] = gts[half].T
        sub = lax.broadcasted_iota(I32, (P_KEYS, nsel), 0).astype(F32)
        zero = jnp.zeros((P_KEYS, nsel), F32)

        def pair_body(m, carry):
            n0 = 2 * m
            onehots = []
            gated = []
            for q in range(2):
                e1 = tm1[pl.ds(n0 + q, 1), :]
                e2 = tm2[pl.ds(n0 + q, 1), :]
                gt = tmg[pl.ds(n0 + q, 1), :]
                gated.append(jnp.where(sub == e1, gt, 0.0))
                onehots.append(jnp.where(sub == e2, 1.0, 0.0))
            a_cat = jnp.concatenate(gated, axis=1).astype(BF16)
            b_bd = jnp.concatenate(
                [jnp.concatenate([onehots[0], zero], axis=1),
                 jnp.concatenate([zero, onehots[1]], axis=1)], axis=0).astype(BF16)
            gn = lax.dot_general(a_cat, b_bd, (((1,), (1,)), ((), ())),
                                 preferred_element_type=F32)
            bits = pltpu.bitcast(gn, I32) + 0x8000
            packed = (bits[half_keys:, :] & -65536) | lax.shift_right_logical(bits[:half_keys, :], 16)
            for q in range(2):
                r0 = pl.multiple_of((n0 + q) * G_PITCH, 8)
                gs_ref[pl.ds(r0, half_keys), :] = packed[:, q * P_KEYS:(q + 1) * P_KEYS]
            return carry

        lax.fori_loop(0, tt // 2, pair_body, 0, unroll=16)
        scores(h1bn_ref)

    nper = 2 * P_HEADS // (half_keys // ngrp)
    units = [_head_topk_steps((j * nper + u) // 2, (j * nper + u) % 2, s_scr, e1s, e2s, gts,
                              key_idx, cand_idx, cand_dup) for u in range(nper)]
    npiece = 2 * (ngrp // 2) + 2
    per_piece = -(-3 * P_TOPK * nper // npiece)

    def advance(count):
        for _ in range(count):
            while units:
                try:
                    next(units[0])
                    break
                except StopIteration:
                    units.pop(0)

    x = h1b_ref[...]
    sub = 2 * P_KEYS
    h_lo, h_hi = [], []
    for k in range(ngrp // 2):
        words = [gs_ref[pl.ds(j * ngrp + 2 * k + q, tt, stride=G_PITCH), :] for q in range(2)]
        cols = slice(k * sub, (k + 1) * sub)
        g_lo = jnp.concatenate([pltpu.bitcast(w << 16, F32) for w in words], axis=1)
        s_lo = jnp.dot(x, ulo_ref[:, cols], preferred_element_type=F32)
        h_lo.append((g_lo * _gelu(s_lo)).astype(BF16))
        advance(per_piece)
        g_hi = jnp.concatenate([pltpu.bitcast(w & -65536, F32) for w in words], axis=1)
        s_hi = jnp.dot(x, uhi_ref[:, cols], preferred_element_type=F32)
        h_hi.append((g_hi * _gelu(s_hi)).astype(BF16))
        advance(per_piece)
    acc = acc_ref[...]
    acc = acc + jnp.dot(jnp.concatenate(h_lo, axis=1), vlo_ref[...], preferred_element_type=F32)
    advance(per_piece)
    acc = acc + jnp.dot(jnp.concatenate(h_hi, axis=1), vhi_ref[...], preferred_element_type=F32)
    advance(4 * P_TOPK * nper)
    acc_ref[...] = acc

    @pl.when(j == pl.num_programs(1) - 1)
    def _():
        out_ref[...] = _layer_norm(ALPHA * h1_ref[...] + acc_ref[...], g_ref[...], b_ref[...])


def peer_fused(h1, h1b, ws, ubt, vb, g, b, tt=512, ngrp=4):
    n = h1.shape[0]
    nsel = P_HEADS * P_TOPK
    nrow = P_HEADS * 2 * P_KEYS
    ntile = n // tt
    nstep = P_KEYS // 2 // ngrp
    assert (2 * P_HEADS) % nstep == 0, "each step runs a whole number of (head, token half) top-k units"
    rows = ngrp * P_KEYS
    th = tt // 2
    tok = lambda i, j: (i, 0)
    full = lambda i, j: (0, 0)
    once = pl.Buffered(1)
    sel = pltpu.VMEM((2, nsel, th), F32)
    tmaj = pltpu.VMEM((tt, nsel), F32)
    return pl.pallas_call(
        functools.partial(_peer_fused_kernel, tt=tt, ngrp=ngrp),
        grid=(ntile, nstep),
        in_specs=[
            pl.BlockSpec((tt, D_MODEL), tok, pipeline_mode=once),
            pl.BlockSpec((tt, D_MODEL), tok, pipeline_mode=once),
            pl.BlockSpec((tt, D_MODEL), lambda i, j: (jnp.minimum(i + 1, ntile - 1), 0), pipeline_mode=once),
            pl.BlockSpec((nrow, D_MODEL), full, pipeline_mode=once),
            pl.BlockSpec((D_MODEL, rows), lambda i, j: (0, j)),
            pl.BlockSpec((D_MODEL, rows), lambda i, j: (0, j + nstep)),
            pl.BlockSpec((rows, D_MODEL), lambda i, j: (j, 0)),
            pl.BlockSpec((rows, D_MODEL), lambda i, j: (j + nstep, 0)),
            pl.BlockSpec((1, D_MODEL), full),
            pl.BlockSpec((1, D_MODEL), full),
        ],
        out_specs=pl.BlockSpec((tt, D_MODEL), tok),
        out_shape=jax.ShapeDtypeStruct((n, D_MODEL), F32),
        scratch_shapes=[
            pltpu.VMEM((tt * G_PITCH, P_KEYS), I32),
            pltpu.VMEM((tt, D_MODEL), F32),
            pltpu.VMEM((2, nrow, th), F32),
            sel, sel, sel,
            tmaj, tmaj, tmaj,
        ],
        compiler_params=pltpu.CompilerParams(
            dimension_semantics=("arbitrary", "arbitrary"), vmem_limit_bytes=VMEM_LIMIT),
        name="peer_fused",
    )(h1, h1b, h1b, ws, ubt, ubt, vb, vb, g, b)


def _pad_cols(w, width):
    return jnp.pad(w, ((0, 0), (0, width - w.shape[1])))


def _pad_rows(w, height):
    return jnp.pad(w, ((0, height - w.shape[0]), (0, 0)))


def _layout_a(w):
    w3 = 3 * A_WIDTH
    return jnp.concatenate([
        w[:, 0:w3], _pad_cols(w[:, w3:w3 + 64], LANES), _pad_cols(w[:, w3 + 64:w3 + 128], LANES),
        w[:, w3 + 128:A_IN]], axis=1)


def _layout_b(w):
    return jnp.concatenate([
        w[:, 0:B_QKV], _pad_cols(w[:, B_QKV:B_QKV + B_GATE_LORA], LANES),
        w[:, B_QKV + B_GATE_LORA:]], axis=1)


def kernel(x, ln_in_g, ln_in_b, w_in, a_mu, a_w0, a_w2, a_a0, a_a2, a_g2, a_k_k, a_k_a, a_r_k,
           a_gn_g, a_gn_b, b_conv, b_gk_w2, b_gk_b, b_norm_g, w_out, ln1_g, ln1_b, p_wq, p_keys,
           p_u, p_v, ln2_g, ln2_b):
    bsz, t, d = x.shape
    n = bsz * t
    row = lambda a: a.reshape(1, -1)

    h = x.reshape(n, d)
    for l in range(DEPTH):
        wa = _layout_a(w_in[l][:, :A_IN]).astype(BF16)
        wb = _layout_b(w_in[l][:, A_IN:]).astype(BF16)
        if l == 0:
            h, pa, pb = ln_proj(h, row(ln_in_g), row(ln_in_b), wa, wb)
        ya = rwkv7(
            pa.reshape(bsz, t, A_PAD), _layout_a(row(a_mu[l])), row(a_w0[l]),
            _pad_rows(a_w2[l], LANES), row(a_a0[l]), _pad_rows(a_a2[l], LANES), a_g2[l],
            row(a_k_k[l]), row(a_k_a[l]), row(a_r_k[l]), row(a_gn_g[l]), row(a_gn_b[l]))
        yb = gla(pb.reshape(bsz, t, B_PAD), b_conv[l], _pad_rows(b_gk_w2[l], LANES),
                 row(b_gk_b[l]), row(b_norm_g[l]))
        h1, h1b = out_ln(ya.reshape(n, A_WIDTH), yb.reshape(n, B_WIDTH), h,
                         w_out[l][:A_WIDTH].astype(BF16), w_out[l][A_WIDTH:].astype(BF16),
                         row(ln1_g[l]), row(ln1_b[l]))
        ws = fold_keys(p_keys[l].reshape(P_HEADS * 2, P_KEYS, P_HALF), p_wq[l])
        h = peer_fused(h1, h1b, ws, p_u[l].astype(BF16).T, p_v[l].astype(BF16),
                       row(ln2_g[l]), row(ln2_b[l]))
    return h.reshape(bsz, t, d)
```

```python
import functools
import math

import jax
import jax.numpy as jnp
from jax import lax
from jax.experimental import pallas as pl
from jax.experimental.pallas import tpu as pltpu

F32 = jnp.float32
BF16 = jnp.bfloat16
I32 = jnp.int32

D_MODEL = 1024
DEPTH = 1
CHUNK = 64
LN_EPS = 1e-5
ALPHA = (2.0 * DEPTH) ** 0.25

A_HEAD = 64
A_WIDTH = 512
A_HEADS = 8
A_GN_EPS = 64e-5
A_IN = 1792
A_PAD = 1920

B_HEADS = 4
B_DV = 128
B_DK = 64
B_WIDTH = 512
B_WIDTH_K = 256
B_QKV = 1024
B_GATE_LORA = 16
B_GATE_NORM = 16.0
B_CONV = 4
B_RMS_EPS = 1e-5
B_PAD = 1664

P_HEADS = 8
P_KEYS = 128
P_HALF = 128
P_TOPK = 16
P_EXPERTS = P_KEYS * P_KEYS

LANES = 128
G_PITCH = 72

VMEM_LIMIT = 56 * 1024 * 1024


def _dot(a, b):
    return jnp.dot(a.astype(BF16), b.astype(BF16), preferred_element_type=F32)


def _dot_nt(a, b):
    return lax.dot_general(a.astype(BF16), b.astype(BF16), (((1,), (1,)), ((), ())),
                           preferred_element_type=F32)


def _dot_tn(a, b):
    return lax.dot_general(a.astype(BF16), b.astype(BF16), (((0,), (0,)), ((), ())),
                           preferred_element_type=F32)


def _split3(x):
    hi = x.astype(BF16)
    r1 = x - hi.astype(F32)
    mid = r1.astype(BF16)
    lo = (r1 - mid.astype(F32)).astype(BF16)
    return hi, mid, lo


def _dot_exact_rhs(a01, x):
    a = a01.astype(BF16)
    hi, mid, lo = _split3(x)
    out = jnp.dot(a, hi, preferred_element_type=F32)
    out += jnp.dot(a, mid, preferred_element_type=F32)
    out += jnp.dot(a, lo, preferred_element_type=F32)
    return out


def _dot_exact_lhs(x, b01):
    b = b01.astype(BF16)
    hi, mid, lo = _split3(x)
    out = jnp.dot(hi, b, preferred_element_type=F32)
    out += jnp.dot(mid, b, preferred_element_type=F32)
    out += jnp.dot(lo, b, preferred_element_type=F32)
    return out


def _sigmoid(x):
    return 1.0 / (1.0 + jnp.exp(-x))


def _softplus(x):
    return jnp.maximum(x, 0.0) + jnp.log(1.0 + jnp.exp(-jnp.abs(x)))


def _silu(x):
    return x * _sigmoid(x)


def _layer_norm(x, g, b):
    mu = jnp.mean(x, axis=-1, keepdims=True)
    xc = x - mu
    var = jnp.mean(xc * xc, axis=-1, keepdims=True)
    return xc * lax.rsqrt(var + LN_EPS) * g + b


def _block_diag_ones(n, seg):
    r = lax.broadcasted_iota(I32, (n, n), 0) // seg
    c = lax.broadcasted_iota(I32, (n, n), 1) // seg
    return jnp.where(r == c, 1.0, 0.0).astype(F32)


def _chunk_tril(n, seg):
    r = lax.broadcasted_iota(I32, (n, n), 0)
    c = lax.broadcasted_iota(I32, (n, n), 1)
    return jnp.where((r // seg == c // seg) & (c <= r), 1.0, 0.0).astype(F32)


def _ln_proj_kernel(x_ref, g_ref, b_ref, wa_ref, wb_ref, h_ref, pa_ref, pb_ref):
    h = _layer_norm(x_ref[...], g_ref[...], b_ref[...])
    h_ref[...] = h
    hb = h.astype(BF16)
    pa_ref[...] = jnp.dot(hb, wa_ref[...], preferred_element_type=F32)
    pb_ref[...] = jnp.dot(hb, wb_ref[...], preferred_element_type=F32)


def ln_proj(x2, g, b, wa, wb, tm=512):
    n = x2.shape[0]
    full = lambda i: (0, 0)
    return pl.pallas_call(
        _ln_proj_kernel,
        grid=(n // tm,),
        in_specs=[
            pl.BlockSpec((tm, D_MODEL), lambda i: (i, 0)),
            pl.BlockSpec((1, D_MODEL), full),
            pl.BlockSpec((1, D_MODEL), full),
            pl.BlockSpec((D_MODEL, A_PAD), full),
            pl.BlockSpec((D_MODEL, B_PAD), full),
        ],
        out_specs=[
            pl.BlockSpec((tm, D_MODEL), lambda i: (i, 0)),
            pl.BlockSpec((tm, A_PAD), lambda i: (i, 0)),
            pl.BlockSpec((tm, B_PAD), lambda i: (i, 0)),
        ],
        out_shape=[
            jax.ShapeDtypeStruct((n, D_MODEL), F32),
            jax.ShapeDtypeStruct((n, A_PAD), F32),
            jax.ShapeDtypeStruct((n, B_PAD), F32),
        ],
        compiler_params=pltpu.CompilerParams(
            dimension_semantics=("arbitrary",), vmem_limit_bytes=VMEM_LIMIT),
        name="ln_proj",
    )(x2, g, b, wa, wb)


def _rwkv_kernel(pa_ref, mu_ref, w0_ref, w2_ref, a0_ref, a2_ref, g2_ref, kkw_ref, kaw_ref,
                 rk_ref, gng_ref, gnb_ref, ya_ref,
                 carry_ref, z_ref, at_s, rt_s, bt_s, kt_s, bh_s, kh_s, v_s, el_s, y_s, *, tb):
    nck = tb // CHUNK
    w = A_WIDTH

    @pl.when(pl.program_id(1) == 0)
    def _():
        carry_ref[...] = jnp.zeros_like(carry_ref)
        z_ref[...] = jnp.zeros_like(z_ref)

    p = pa_ref[0]
    rolled = pltpu.roll(p, 1, axis=0)
    row = lax.broadcasted_iota(I32, p.shape, 0)
    prev = jnp.where(row == 0, carry_ref[0:1, :], rolled)
    carry_ref[0:1, :] = p[tb - 1:tb, :]
    ps = p + (prev - p) * mu_ref[...]

    r = ps[:, 0:w]
    k = ps[:, w:2 * w]
    v = ps[:, 2 * w:3 * w]
    wd = ps[:, 3 * w:3 * w + LANES]
    ad = ps[:, 3 * w + LANES:3 * w + 2 * LANES]
    gd = ps[:, 3 * w + 2 * LANES:3 * w + 3 * LANES]

    wlin = w0_ref[...] + _dot(jnp.tanh(wd), w2_ref[...])
    lw = -jnp.exp(-_softplus(-wlin) - 0.5)
    iclr = _sigmoid(a0_ref[...] + _dot(ad, a2_ref[...]))
    gate = _dot(_sigmoid(gd), g2_ref[...])

    bd = _block_diag_ones(w, A_HEAD)
    kkr = k * kkw_ref[...]
    ss = _dot_exact_lhs(kkr * kkr, bd)
    kk = kkr / jnp.maximum(jnp.sqrt(ss), 1e-12)
    kh = k * (1.0 + (iclr - 1.0) * kaw_ref[...])
    b = kk * iclr
    bonus = _dot_exact_lhs(r * kh * rk_ref[...], bd) * v

    lcum = _dot_exact_rhs(_chunk_tril(tb, CHUNK), lw)
    e_l = jnp.exp(lcum)
    e_nl = jnp.exp(-lcum)
    at_s[...] = -kk * jnp.exp(lcum - lw)
    rt_s[...] = r * e_l
    bt_s[...] = b * e_nl
    kt_s[...] = kh * e_nl
    v_s[...] = v
    for c in range(nck):
        rows = slice(c * CHUNK, (c + 1) * CHUNK)
        l_end = lcum[(c + 1) * CHUNK - 1:(c + 1) * CHUNK, :]
        e_lc = jnp.exp(l_end - lcum[rows, :])
        bh_s[rows, :] = b[rows, :] * e_lc
        kh_s[rows, :] = kh[rows, :] * e_lc
        el_s[c:c + 1, :] = jnp.exp(l_end)

    ri = lax.broadcasted_iota(I32, (2 * CHUNK, 2 * CHUNK), 0)
    ci = lax.broadcasted_iota(I32, (2 * CHUNK, 2 * CHUNK), 1) % CHUNK
    tri_mask = ci < jnp.where(ri < CHUNK, ri, ri - (CHUNK - 1))
    er = lax.broadcasted_iota(I32, (CHUNK, CHUNK), 0)
    ec = lax.broadcasted_iota(I32, (CHUNK, CHUNK), 1)
    eye_mask = er == ec
    eye = jnp.where(eye_mask, 1.0, 0.0).astype(F32)

    heads = range(A_HEADS)
    chains = [(c, h) for c in range(nck) for h in heads]
    rows = [slice(c * CHUNK, (c + 1) * CHUNK) for c in range(nck)]
    slabs = [{name: ref[rows[c], :] for name, ref in
              (("at", at_s), ("rt", rt_s), ("bt", bt_s), ("kt", kt_s), ("bh", bh_s), ("kh", kh_s), ("v", v_s))}
             for c in range(nck)]
    els = [el_s[c:c + 1, :] for c in range(nck)]

    def part(name, c, h):
        return slabs[c][name][:, h * A_HEAD:(h + 1) * A_HEAD]

    gms = [jnp.where(tri_mask,
                     _dot_nt(jnp.concatenate([part("at", c, h), part("rt", c, h)], axis=0),
                             jnp.concatenate([part("bt", c, h), part("kt", c, h)], axis=0)), 0.0)
           for c, h in chains]
    mvs = [_dot(gm[:, CHUNK:2 * CHUNK], part("v", c, h)) for gm, (c, h) in zip(gms, chains)]
    khv = [_dot_tn(part("kh", c, h), part("v", c, h)) for c, h in chains]
    pws = [gm[0:CHUNK, 0:CHUNK] for gm in gms]
    tms = [eye + pw for pw in pws]
    for _ in range(5):
        pws = [_dot(pw, pw) for pw in pws]
        tms = [tm + _dot(tm, pw) for tm, pw in zip(tms, pws)]
    p12s = [_dot(tm, jnp.concatenate([mv[0:CHUNK], part("at", c, h)], axis=1))
            for tm, mv, (c, h) in zip(tms, mvs, chains)]
    yqs = [jnp.concatenate([mv[CHUNK:2 * CHUNK], part("rt", c, h)], axis=1)
           + _dot(gm[CHUNK:2 * CHUNK, 0:CHUNK], p12)
           for mv, gm, p12, (c, h) in zip(mvs, gms, p12s, chains)]
    zws = [jnp.concatenate(
               [kv, jnp.where(eye_mask,
                              jnp.broadcast_to(els[c][:, h * A_HEAD:(h + 1) * A_HEAD], (CHUNK, CHUNK)), 0.0)],
               axis=1)
           + _dot_tn(part("bh", c, h), p12)
           for kv, p12, (c, h) in zip(khv, p12s, chains)]

    zs = [z_ref[h] for h in heads]
    for c in range(nck):
        ys = []
        for h in heads:
            yq, zw = yqs[c * A_HEADS + h], zws[c * A_HEADS + h]
            ys.append(yq[:, 0:CHUNK] + _dot(yq[:, CHUNK:2 * CHUNK], zs[h]))
            zs[h] = zw[:, 0:CHUNK] + _dot(zw[:, CHUNK:2 * CHUNK], zs[h])
        y_s[rows[c], :] = jnp.concatenate(ys, axis=1)
    for h in heads:
        z_ref[h] = zs[h]

    y = y_s[...]
    inv_n = 1.0 / A_HEAD
    mean = _dot_exact_lhs(y, bd) * inv_n
    yc = y - mean
    var = _dot_exact_lhs(yc * yc, bd) * inv_n
    yn = yc * lax.rsqrt(var + A_GN_EPS) * gng_ref[...] + gnb_ref[...]
    ya_ref[0] = (yn + bonus) * gate


def rwkv7(pa3, mu, w0, w2, a0, a2, g2, kkw, kaw, rk, gng, gnb, tb=256):
    bsz, t, _ = pa3.shape
    full = lambda bi, ti: (0, 0)
    vec = pl.BlockSpec((1, A_WIDTH), full)
    slab = pltpu.VMEM((tb, A_WIDTH), F32)
    return pl.pallas_call(
        functools.partial(_rwkv_kernel, tb=tb),
        grid=(bsz, t // tb),
        in_specs=[
            pl.BlockSpec((1, tb, A_PAD), lambda bi, ti: (bi, ti, 0)),
            pl.BlockSpec((1, A_PAD), full),
            vec,
            pl.BlockSpec((LANES, A_WIDTH), full),
            vec,
            pl.BlockSpec((LANES, A_WIDTH), full),
            pl.BlockSpec((LANES, A_WIDTH), full),
            vec, vec, vec, vec, vec,
        ],
        out_specs=pl.BlockSpec((1, tb, A_WIDTH), lambda bi, ti: (bi, ti, 0)),
        out_shape=jax.ShapeDtypeStruct((bsz, t, A_WIDTH), F32),
        scratch_shapes=[
            pltpu.VMEM((8, A_PAD), F32),
            pltpu.VMEM((A_HEADS, A_HEAD, A_HEAD), F32),
            slab, slab, slab, slab, slab, slab, slab,
            pltpu.VMEM((8, A_WIDTH), F32),
            slab,
        ],
        compiler_params=pltpu.CompilerParams(
            dimension_semantics=("arbitrary", "arbitrary"), vmem_limit_bytes=VMEM_LIMIT),
        name="rwkv7",
    )(pa3, mu, w0, w2, a0, a2, g2, kkw, kaw, rk, gng, gnb)


def _gla_kernel(pb_ref, cw_ref, gw_ref, gb_ref, ng_ref, yb_ref,
                xpad_ref, s_ref, qt_s, kt_s, kh_s, v_s, gk_s, o_s, *, tb):
    nck = tb // CHUNK

    @pl.when(pl.program_id(1) == 0)
    def _():
        xpad_ref[0:8, :] = jnp.zeros((8, B_QKV), F32)
        s_ref[...] = jnp.zeros_like(s_ref)

    p = pb_ref[0]
    xpad_ref[8:8 + tb, :] = p[:, 0:B_QKV]
    conv = cw_ref[B_CONV - 1:B_CONV, :] * p[:, 0:B_QKV]
    for j in range(B_CONV - 1):
        sh = B_CONV - 1 - j
        conv = conv + cw_ref[j:j + 1, :] * xpad_ref[8 - sh:8 - sh + tb, :]
    xpad_ref[0:8, :] = p[tb - 8:tb, 0:B_QKV]
    qkv = _silu(conv)
    q = qkv[:, 0:B_WIDTH_K] * (B_DK ** -0.5)
    k = qkv[:, B_WIDTH_K:2 * B_WIDTH_K]
    v_s[...] = qkv[:, 2 * B_WIDTH_K:]
    gkd = p[:, B_QKV:B_QKV + LANES]
    gz = _dot(gkd, gw_ref[...]) + gb_ref[...]
    gk = -_softplus(-gz) * (1.0 / B_GATE_NORM)
    gk_s[...] = gk
    gcum = _dot_exact_rhs(_chunk_tril(tb, CHUNK), gk)
    qt_s[...] = q * jnp.exp(gcum)
    kt_s[...] = k * jnp.exp(-gcum)
    for c in range(nck):
        rows = slice(c * CHUNK, (c + 1) * CHUNK)
        g_end = gcum[(c + 1) * CHUNK - 1:(c + 1) * CHUNK, :]
        kh_s[rows, :] = k[rows, :] * jnp.exp(g_end - gcum[rows, :])

    ri = lax.broadcasted_iota(I32, (CHUNK, CHUNK), 0)
    ci = lax.broadcasted_iota(I32, (CHUNK, CHUNK), 1)
    causal = ci <= ri
    ones_cv = jnp.ones((CHUNK, B_DV), F32)

    heads = range(B_HEADS)
    chains = [(c, h) for c in range(nck) for h in heads]
    rows = [slice(c * CHUNK, (c + 1) * CHUNK) for c in range(nck)]
    slabs = [{name: ref[rows[c], :] for name, ref in
              (("qt", qt_s), ("kt", kt_s), ("kh", kh_s), ("v", v_s), ("gk", gk_s))} for c in range(nck)]

    def key_part(name, c, h):
        return slabs[c][name][:, h * B_DK:(h + 1) * B_DK]

    def val_part(c, h):
        return slabs[c]["v"][:, h * B_DV:(h + 1) * B_DV]

    ob = ones_cv.astype(BF16)
    dn = (((0,), (0,)), ((), ()))
    intra = [_dot(jnp.where(causal, _dot_nt(key_part("qt", c, h), key_part("kt", c, h)), 0.0), val_part(c, h))
             for c, h in chains]
    kvs = [_dot_tn(key_part("kh", c, h), val_part(c, h)) for c, h in chains]
    decays = []
    for c, h in chains:
        hi, mid, lo = _split3(key_part("gk", c, h))
        gcol = (lax.dot_general(hi, ob, dn, preferred_element_type=F32)
                + lax.dot_general(mid, ob, dn, preferred_element_type=F32)
                + lax.dot_general(lo, ob, dn, preferred_element_type=F32))
        decays.append(jnp.exp(gcol))
    ss = [s_ref[h] for h in heads]
    for c in range(nck):
        os_ = []
        for h in heads:
            n = c * B_HEADS + h
            os_.append(intra[n] + _dot(key_part("qt", c, h), ss[h]))
            ss[h] = ss[h] * decays[n] + kvs[n]
        o_s[rows[c], :] = jnp.concatenate(os_, axis=1)
    for h in heads:
        s_ref[h] = ss[h]

    g = p[:, B_QKV + LANES:]
    outs = []
    for h in range(B_HEADS):
        sv = slice(h * B_DV, (h + 1) * B_DV)
        o = o_s[:, sv]
        o = o * lax.rsqrt(jnp.mean(o * o, axis=-1, keepdims=True) + B_RMS_EPS) * ng_ref[...]
        outs.append(o * _silu(g[:, sv]))
    yb_ref[0] = jnp.concatenate(outs, axis=1)


def gla(pb3, cw, gw, gb, ng, tb=256):
    bsz, t, _ = pb3.shape
    full = lambda bi, ti: (0, 0)
    return pl.pallas_call(
        functools.partial(_gla_kernel, tb=tb),
        grid=(bsz, t // tb),
        in_specs=[
            pl.BlockSpec((1, tb, B_PAD), lambda bi, ti: (bi, ti, 0)),
            pl.BlockSpec((B_CONV, B_QKV), full),
            pl.BlockSpec((LANES, B_WIDTH_K), full),
            pl.BlockSpec((1, B_WIDTH_K), full),
            pl.BlockSpec((1, B_DV), full),
        ],
        out_specs=pl.BlockSpec((1, tb, B_WIDTH), lambda bi, ti: (bi, ti, 0)),
        out_shape=jax.ShapeDtypeStruct((bsz, t, B_WIDTH), F32),
        scratch_shapes=[
            pltpu.VMEM((tb + 8, B_QKV), F32),
            pltpu.VMEM((B_HEADS, B_DK, B_DV), F32),
            pltpu.VMEM((tb, B_WIDTH_K), F32),
            pltpu.VMEM((tb, B_WIDTH_K), F32),
            pltpu.VMEM((tb, B_WIDTH_K), F32),
            pltpu.VMEM((tb, B_WIDTH), F32),
            pltpu.VMEM((tb, B_WIDTH_K), F32),
            pltpu.VMEM((tb, B_WIDTH), F32),
        ],
        compiler_params=pltpu.CompilerParams(
            dimension_semantics=("arbitrary", "arbitrary"), vmem_limit_bytes=VMEM_LIMIT),
        name="gla",
    )(pb3, cw, gw, gb, ng)


def _out_ln_kernel(ya_ref, yb_ref, h_ref, wa_ref, wb_ref, g_ref, b_ref, h1_ref, h1b_ref):
    y = _dot(ya_ref[...], wa_ref[...]) + _dot(yb_ref[...], wb_ref[...])
    h1 = _layer_norm(ALPHA * h_ref[...] + y, g_ref[...], b_ref[...])
    h1_ref[...] = h1
    h1b_ref[...] = h1.astype(BF16)


def out_ln(ya, yb, h, wa, wb, g, b, tm=512):
    n = h.shape[0]
    full = lambda i: (0, 0)
    return pl.pallas_call(
        _out_ln_kernel,
        grid=(n // tm,),
        in_specs=[
            pl.BlockSpec((tm, A_WIDTH), lambda i: (i, 0)),
            pl.BlockSpec((tm, B_WIDTH), lambda i: (i, 0)),
            pl.BlockSpec((tm, D_MODEL), lambda i: (i, 0)),
            pl.BlockSpec((A_WIDTH, D_MODEL), full),
            pl.BlockSpec((B_WIDTH, D_MODEL), full),
            pl.BlockSpec((1, D_MODEL), full),
            pl.BlockSpec((1, D_MODEL), full),
        ],
        out_specs=[
            pl.BlockSpec((tm, D_MODEL), lambda i: (i, 0)),
            pl.BlockSpec((tm, D_MODEL), lambda i: (i, 0)),
        ],
        out_shape=[
            jax.ShapeDtypeStruct((n, D_MODEL), F32),
            jax.ShapeDtypeStruct((n, D_MODEL), BF16),
        ],
        compiler_params=pltpu.CompilerParams(
            dimension_semantics=("arbitrary",), vmem_limit_bytes=VMEM_LIMIT),
        name="out_ln",
    )(ya, yb, h, wa, wb, g, b)


def _fold_keys_kernel(keys_ref, wq_ref, out_ref):
    kh, km, kl = _split3(keys_ref[0])
    wh, wm, wl = _split3(wq_ref[...])
    dn = (((1,), (1,)), ((), ()))
    acc = lax.dot_general(kh, wh, dn, preferred_element_type=F32)
    for a, b in ((kh, wm), (km, wh), (km, wm), (kh, wl), (kl, wh)):
        acc += lax.dot_general(a, b, dn, preferred_element_type=F32)
    out_ref[...] = acc.astype(BF16)


def fold_keys(keys3, wq):
    nhp = keys3.shape[0]
    return pl.pallas_call(
        _fold_keys_kernel,
        grid=(nhp,),
        in_specs=[
            pl.BlockSpec((1, P_KEYS, P_HALF), lambda i: (i, 0, 0)),
            pl.BlockSpec((D_MODEL, P_HALF), lambda i: (0, i)),
        ],
        out_specs=pl.BlockSpec((P_KEYS, D_MODEL), lambda i: (i, 0)),
        out_shape=jax.ShapeDtypeStruct((nhp * P_KEYS, D_MODEL), BF16),
        compiler_params=pltpu.CompilerParams(dimension_semantics=("arbitrary",)),
        name="fold_keys",
    )(keys3, wq)


def _top16(vals, idx):
    out_v, out_i = [], []
    for _ in range(P_TOPK):
        m = jnp.max(vals, axis=0, keepdims=True)
        i = jnp.min(jnp.where(vals == m, idx, float(2 ** 20)), axis=0, keepdims=True)
        out_v.append(m)
        out_i.append(i)
        vals = jnp.where(idx == i, -jnp.inf, vals)
    return jnp.concatenate(out_v, axis=0), jnp.concatenate(out_i, axis=0)


def _cand_index(tt):
    r = lax.broadcasted_iota(I32, (72, tt), 0)
    c = jnp.where(r < 24, r, jnp.where(r < 32, r + 8, r + 16))
    c = jnp.where(r >= 40, (r - 40) * P_TOPK, c)
    c = jnp.where(r >= 56, (r - 56) * P_TOPK + 1, c)
    c = jnp.where(r >= 64, (r - 64) * P_TOPK + 2, c)
    pos = jnp.where(r >= 64, r - 64, jnp.where(r >= 56, r - 56, jnp.where(r >= 40, r - 40, 4)))
    return c.astype(F32), pos < 4


def _pick16(table, sel):
    out = jnp.zeros_like(table)
    for i in range(P_TOPK):
        out = jnp.where(sel == float(i), table[i:i + 1, :], out)
    return out


def _peer_topk_kernel(ws_ref, h_ref, e1_ref, e2_ref, gt_ref, s_ref):
    s_ref[...] = lax.dot_general(ws_ref[...], h_ref[...], (((1,), (1,)), ((), ())),
                                 preferred_element_type=F32)

    tt = s_ref.shape[1]
    key_idx = lax.broadcasted_iota(I32, (P_KEYS, tt), 0).astype(F32)
    cand_idx, cand_dup = _cand_index(tt)

    def head_body(h, carry):
        r0 = pl.multiple_of(h * 2 * P_KEYS, 2 * P_KEYS)
        sv1, si1 = _top16(s_ref[pl.ds(r0, P_KEYS), :], key_idx)
        sv2, si2 = _top16(s_ref[pl.ds(r0 + P_KEYS, P_KEYS), :], key_idx)
        cand = jnp.concatenate(
            [sv1[0:1, :] + sv2]
            + [sv1[i:i + 1, :] + sv2[0:8, :] for i in (1, 2, 3)]
            + [sv1 + sv2[0:1, :]]
            + [sv1[0:8, :] + sv2[j:j + 1, :] for j in (1, 2)], axis=0)
        cv, cidx = _top16(jnp.where(cand_dup, -jnp.inf, cand), cand_idx)
        qi = jnp.floor(cidx * (1.0 / P_TOPK))
        ri = cidx - qi * P_TOPK
        e1 = _pick16(si1, qi)
        e2 = _pick16(si2, ri)
        ex = jnp.exp(cv - cv[0:1, :])
        gates = ex / jnp.sum(ex, axis=0, keepdims=True)
        o0 = pl.multiple_of(h * P_TOPK, P_TOPK)
        e1_ref[pl.ds(o0, P_TOPK), :] = e1.astype(I32)
        e2_ref[pl.ds(o0, P_TOPK), :] = e2.astype(I32)
        gt_ref[pl.ds(o0, P_TOPK), :] = gates
        return carry

    lax.fori_loop(0, P_HEADS, head_body, 0)


def peer_topk(ws, h1b, tt=256):
    n = h1b.shape[0]
    nsel = P_HEADS * P_TOPK
    nrow = P_HEADS * 2 * P_KEYS
    out = jax.ShapeDtypeStruct((nsel, n), I32)
    spec = pl.BlockSpec((nsel, tt), lambda i: (0, i))
    return pl.pallas_call(
        _peer_topk_kernel,
        grid=(n // tt,),
        in_specs=[
            pl.BlockSpec((nrow, D_MODEL), lambda i: (0, 0)),
            pl.BlockSpec((tt, D_MODEL), lambda i: (i, 0)),
        ],
        out_specs=[spec, spec, spec],
        out_shape=[out, out, jax.ShapeDtypeStruct((nsel, n), F32)],
        scratch_shapes=[pltpu.VMEM((nrow, tt), F32)],
        compiler_params=pltpu.CompilerParams(
            dimension_semantics=("arbitrary",), vmem_limit_bytes=VMEM_LIMIT),
        name="peer_topk",
    )(ws, h1b)


def _gelu(x):
    return 0.5 * x * (1.0 + lax.erf(x * (1.0 / math.sqrt(2.0))))


def _gelu_x2(x):
    return x * (1.0 + lax.erf(x * (1.0 / math.sqrt(2.0))))


def _peer_ffn_kernel(h1_ref, h1b_ref, e1_ref, e2_ref, gt_ref, ulo_ref, uhi_ref, vlo_ref, vhi_ref,
                     g_ref, b_ref, out_ref, gs_ref, acc_ref, *, tt, ngrp):
    j = pl.program_id(1)
    nsel = P_HEADS * P_TOPK
    half = P_KEYS // 2

    @pl.when(j == 0)
    def _():
        acc_ref[...] = jnp.zeros_like(acc_ref)
        sub = lax.broadcasted_iota(I32, (P_KEYS, nsel), 0)
        zero = jnp.zeros((P_KEYS, nsel), F32)

        def pair_body(m, carry):
            n0 = 2 * m
            onehots = []
            gated = []
            for q in range(2):
                e1 = e1_ref[pl.ds(n0 + q, 1), :]
                e2 = e2_ref[pl.ds(n0 + q, 1), :]
                gt = gt_ref[pl.ds(n0 + q, 1), :]
                gated.append(jnp.where(sub == e1, gt, 0.0))
                onehots.append(jnp.where(sub == e2, 1.0, 0.0))
            a_cat = jnp.concatenate(gated, axis=1).astype(BF16)
            b_bd = jnp.concatenate(
                [jnp.concatenate([onehots[0], zero], axis=1),
                 jnp.concatenate([zero, onehots[1]], axis=1)], axis=0).astype(BF16)
            gn = lax.dot_general(a_cat, b_bd, (((1,), (1,)), ((), ())),
                                 preferred_element_type=F32)
            bits = pltpu.bitcast(gn, I32) + 0x8000
            packed = (bits[half:, :] & -65536) | lax.shift_right_logical(bits[:half, :], 16)
            for q in range(2):
                r0 = pl.multiple_of((n0 + q) * G_PITCH, 8)
                gs_ref[pl.ds(r0, half), :] = packed[:, q * P_KEYS:(q + 1) * P_KEYS]
            return carry

        lax.fori_loop(0, tt // 2, pair_body, 0, unroll=16)

    x = h1b_ref[...]
    sub = 2 * P_KEYS
    h_lo, h_hi = [], []
    for i in range(ngrp // 2):
        words = [gs_ref[pl.ds(j * ngrp + 2 * i + q, tt, stride=G_PITCH), :] for q in range(2)]
        g_lo = jnp.concatenate([pltpu.bitcast(w << 16, F32) for w in words], axis=1)
        g_hi = jnp.concatenate([pltpu.bitcast(w & -65536, F32) for w in words], axis=1)
        cols = slice(i * sub, (i + 1) * sub)
        s_lo = jnp.dot(x, ulo_ref[:, cols], preferred_element_type=F32)
        s_hi = jnp.dot(x, uhi_ref[:, cols], preferred_element_type=F32)
        h_lo.append((g_lo * _gelu(s_lo)).astype(BF16))
        h_hi.append((g_hi * _gelu(s_hi)).astype(BF16))
    acc = acc_ref[...]
    acc = acc + jnp.dot(jnp.concatenate(h_lo, axis=1), vlo_ref[...], preferred_element_type=F32)
    acc = acc + jnp.dot(jnp.concatenate(h_hi, axis=1), vhi_ref[...], preferred_element_type=F32)
    acc_ref[...] = acc

    @pl.when(j == pl.num_programs(1) - 1)
    def _():
        out_ref[...] = _layer_norm(ALPHA * h1_ref[...] + acc_ref[...], g_ref[...], b_ref[...])


def peer_ffn(h1, h1b, e1t, e2t, gtt, ubt, vb, g, b, tt=512, ngrp=8):
    n = h1.shape[0]
    nsel = P_HEADS * P_TOPK
    nstep = P_KEYS // 2 // ngrp
    rows = ngrp * P_KEYS
    tok = lambda i, j: (i, 0)
    full = lambda i, j: (0, 0)
    lo = lambda i, j: (j, 0)
    hi = lambda i, j: (j + nstep, 0)
    once = pl.Buffered(1)
    return pl.pallas_call(
        functools.partial(_peer_ffn_kernel, tt=tt, ngrp=ngrp),
        grid=(n // tt, nstep),
        in_specs=[
            pl.BlockSpec((tt, D_MODEL), tok, pipeline_mode=once),
            pl.BlockSpec((tt, D_MODEL), tok, pipeline_mode=once),
            pl.BlockSpec((tt, nsel), tok, pipeline_mode=once),
            pl.BlockSpec((tt, nsel), tok, pipeline_mode=once),
            pl.BlockSpec((tt, nsel), tok, pipeline_mode=once),
            pl.BlockSpec((D_MODEL, rows), lambda i, j: (0, j)),
            pl.BlockSpec((D_MODEL, rows), lambda i, j: (0, j + nstep)),
            pl.BlockSpec((rows, D_MODEL), lo),
            pl.BlockSpec((rows, D_MODEL), hi),
            pl.BlockSpec((1, D_MODEL), full),
            pl.BlockSpec((1, D_MODEL), full),
        ],
        out_specs=pl.BlockSpec((tt, D_MODEL), tok),
        out_shape=jax.ShapeDtypeStruct((n, D_MODEL), F32),
        scratch_shapes=[
            pltpu.VMEM((tt * G_PITCH, P_KEYS), I32),
            pltpu.VMEM((tt, D_MODEL), F32),
        ],
        compiler_params=pltpu.CompilerParams(
            dimension_semantics=("arbitrary", "arbitrary"), vmem_limit_bytes=VMEM_LIMIT),
        name="peer_ffn",
    )(h1, h1b, e1t, e2t, gtt, ubt, ubt, vb, vb, g, b)


def _top16_steps(vals, idx):
    out_v, out_i = [], []
    for _ in range(P_TOPK):
        m = jnp.max(vals, axis=0, keepdims=True)
        i = jnp.min(jnp.where(vals == m, idx, float(2 ** 20)), axis=0, keepdims=True)
        out_v.append(m)
        out_i.append(i)
        vals = jnp.where(idx == i, -jnp.inf, vals)
        yield
    return jnp.concatenate(out_v, axis=0), jnp.concatenate(out_i, axis=0)


def _head_topk_steps(h, half, s_scr, e1s, e2s, gts, key_idx, cand_idx, cand_dup):
    r0 = pl.multiple_of(h * 2 * P_KEYS, 2 * P_KEYS)
    sv1, si1 = yield from _top16_steps(s_scr[half, pl.ds(r0, P_KEYS), :], key_idx)
    sv2, si2 = yield from _top16_steps(s_scr[half, pl.ds(r0 + P_KEYS, P_KEYS), :], key_idx)
    cand = jnp.concatenate(
        [sv1[0:1, :] + sv2]
        + [sv1[i:i + 1, :] + sv2[0:8, :] for i in (1, 2, 3)]
        + [sv1 + sv2[0:1, :]]
        + [sv1[0:8, :] + sv2[j:j + 1, :] for j in (1, 2)], axis=0)
    cv, cidx = yield from _top16_steps(jnp.where(cand_dup, -jnp.inf, cand), cand_idx)
    qi = jnp.floor(cidx * (1.0 / P_TOPK))
    ri = cidx - qi * P_TOPK
    ex = jnp.exp(cv - cv[0:1, :])
    o0 = pl.multiple_of(h * P_TOPK, P_TOPK)
    e1s[half, pl.ds(o0, P_TOPK), :] = _pick16(si1, qi)
    e2s[half, pl.ds(o0, P_TOPK), :] = _pick16(si2, ri)
    gts[half, pl.ds(o0, P_TOPK), :] = ex / jnp.sum(ex, axis=0, keepdims=True)


def _head_topk(*args, **kwargs):
    for _ in _head_topk_steps(*args, **kwargs):
        pass


def _peer_fused_kernel(h1_ref, h1b_ref, h1bn_ref, ws_ref, ulo_ref, uhi_ref, vlo_ref, vhi_ref,
                       g_ref, b_ref, out_ref,
                       gs_ref, acc_ref, s_scr, e1s, e2s, gts, tm1, tm2, tmg, *, tt, ngrp):
    i = pl.program_id(0)
    j = pl.program_id(1)
    nsel = P_HEADS * P_TOPK
    half_keys = P_KEYS // 2
    th = tt // 2
    key_idx = lax.broadcasted_iota(I32, (P_KEYS, th), 0).astype(F32)
    cand_idx, cand_dup = _cand_index(th)
    topk = functools.partial(_head_topk, s_scr=s_scr, e1s=e1s, e2s=e2s, gts=gts,
                             key_idx=key_idx, cand_idx=cand_idx, cand_dup=cand_dup)

    def scores(hb_ref):
        for half in range(2):
            s_scr[half] = lax.dot_general(ws_ref[...], hb_ref[half * th:(half + 1) * th, :],
                                          (((1,), (1,)), ((), ())), preferred_element_type=F32)

    @pl.when((i == 0) & (j == 0))
    def _():
        scores(h1b_ref)

        def body(t, carry):
            topk(t // 2, t % 2)
            return carry

        lax.fori_loop(0, 2 * P_HEADS, body, 0)

    @pl.when(j == 0)
    def _():
        acc_ref[...] = jnp.zeros_like(acc_ref)
        for half in range(2):
            rows = slice(half * th, (half + 1) * th)
            tm1[rows, :] = e1s[half].T
            tm2[rows, :] = e2s[half].T
            tmg[rows, :] = gts[half].T
        sub = lax.broadcasted_iota(I32, (P_KEYS, nsel), 0).astype(F32)
        zero = jnp.zeros((P_KEYS, nsel), F32)

        def pair_body(m, carry):
            n0 = 2 * m
            onehots = []
            gated = []
            for q in range(2):
                e1 = tm1[pl.ds(n0 + q, 1), :]
                e2 = tm2[pl.ds(n0 + q, 1), :]
                gt = 0.5 * tmg[pl.ds(n0 + q, 1), :]
                gated.append(jnp.where(sub == e1, gt, 0.0))
                onehots.append(jnp.where(sub == e2, 1.0, 0.0))
            a_cat = jnp.concatenate(gated, axis=1).astype(BF16)
            b_bd = jnp.concatenate(
                [jnp.concatenate([onehots[0], zero], axis=1),
                 jnp.concatenate([zero, onehots[1]], axis=1)], axis=0).astype(BF16)
            gn = lax.dot_general(a_cat, b_bd, (((1,), (1,)), ((), ())),
                                 preferred_element_type=F32)
            bits = pltpu.bitcast(gn, I32) + 0x8000
            packed = (bits[half_keys:, :] & -65536) | lax.shift_right_logical(bits[:half_keys, :], 16)
            for q in range(2):
                r0 = pl.multiple_of((n0 + q) * G_PITCH, 8)
                gs_ref[pl.ds(r0, half_keys), :] = packed[:, q * P_KEYS:(q + 1) * P_KEYS]
            return carry

        lax.fori_loop(0, tt // 2, pair_body, 0, unroll=16)
        scores(h1bn_ref)

    nper = 2 * P_HEADS // (half_keys // ngrp)
    units = [_head_topk_steps((j * nper + u) // 2, (j * nper + u) % 2, s_scr, e1s, e2s, gts,
                              key_idx, cand_idx, cand_dup) for u in range(nper)]
    msplit = 2
    npiece = msplit * (2 * (ngrp // 2) + 2)
    per_piece = -(-3 * P_TOPK * nper // npiece)
    tm = tt // msplit

    def advance(count):
        for _ in range(count):
            while units:
                try:
                    next(units[0])
                    break
                except StopIteration:
                    units.pop(0)

    sub = 2 * P_KEYS
    for mi in range(msplit):
        trow = slice(mi * tm, (mi + 1) * tm)
        x = h1b_ref[trow, :]
        h_lo, h_hi = [], []
        for k in range(ngrp // 2):
            words = [gs_ref[pl.ds((mi * tm) * G_PITCH + j * ngrp + 2 * k + q, tm, stride=G_PITCH), :]
                     for q in range(2)]
            cols = slice(k * sub, (k + 1) * sub)
            g_lo = jnp.concatenate([pltpu.bitcast(w << 16, F32) for w in words], axis=1)
            s_lo = jnp.dot(x, ulo_ref[:, cols], preferred_element_type=F32)
            h_lo.append((g_lo * _gelu_x2(s_lo)).astype(BF16))
            advance(per_piece)
            g_hi = jnp.concatenate([pltpu.bitcast(w & -65536, F32) for w in words], axis=1)
            s_hi = jnp.dot(x, uhi_ref[:, cols], preferred_element_type=F32)
            h_hi.append((g_hi * _gelu_x2(s_hi)).astype(BF16))
            advance(per_piece)
        acc = acc_ref[trow, :]
        acc = acc + jnp.dot(jnp.concatenate(h_lo, axis=1), vlo_ref[...], preferred_element_type=F32)
        advance(per_piece)
        acc = acc + jnp.dot(jnp.concatenate(h_hi, axis=1), vhi_ref[...], preferred_element_type=F32)
        advance(per_piece)
        acc_ref[trow, :] = acc
    advance(4 * P_TOPK * nper)

    @pl.when(j == pl.num_programs(1) - 1)
    def _():
        out_ref[...] = _layer_norm(ALPHA * h1_ref[...] + acc_ref[...], g_ref[...], b_ref[...])


def peer_fused(h1, h1b, ws, ubt, vb, g, b, tt=512, ngrp=4):
    n = h1.shape[0]
    nsel = P_HEADS * P_TOPK
    nrow = P_HEADS * 2 * P_KEYS
    ntile = n // tt
    nstep = P_KEYS // 2 // ngrp
    assert (2 * P_HEADS) % nstep == 0, "each step runs a whole number of (head, token half) top-k units"
    rows = ngrp * P_KEYS
    th = tt // 2
    tok = lambda i, j: (i, 0)
    full = lambda i, j: (0, 0)
    once = pl.Buffered(1)
    sel = pltpu.VMEM((2, nsel, th), F32)
    tmaj = pltpu.VMEM((tt, nsel), F32)
    return pl.pallas_call(
        functools.partial(_peer_fused_kernel, tt=tt, ngrp=ngrp),
        grid=(ntile, nstep),
        in_specs=[
            pl.BlockSpec((tt, D_MODEL), tok, pipeline_mode=once),
            pl.BlockSpec((tt, D_MODEL), tok, pipeline_mode=once),
            pl.BlockSpec((tt, D_MODEL), lambda i, j: (jnp.minimum(i + 1, ntile - 1), 0), pipeline_mode=once),
            pl.BlockSpec((nrow, D_MODEL), full, pipeline_mode=once),
            pl.BlockSpec((D_MODEL, rows), lambda i, j: (0, j)),
            pl.BlockSpec((D_MODEL, rows), lambda i, j: (0, j + nstep)),
            pl.BlockSpec((rows, D_MODEL), lambda i, j: (j, 0)),
            pl.BlockSpec((rows, D_MODEL), lambda i, j: (j + nstep, 0)),
            pl.BlockSpec((1, D_MODEL), full),
            pl.BlockSpec((1, D_MODEL), full),
        ],
        out_specs=pl.BlockSpec((tt, D_MODEL), tok),
        out_shape=jax.ShapeDtypeStruct((n, D_MODEL), F32),
        scratch_shapes=[
            pltpu.VMEM((tt * G_PITCH, P_KEYS), I32),
            pltpu.VMEM((tt, D_MODEL), F32),
            pltpu.VMEM((2, nrow, th), F32),
            sel, sel, sel,
            tmaj, tmaj, tmaj,
        ],
        compiler_params=pltpu.CompilerParams(
            dimension_semantics=("arbitrary", "arbitrary"), vmem_limit_bytes=VMEM_LIMIT),
        name="peer_fused",
    )(h1, h1b, h1b, ws, ubt, ubt, vb, vb, g, b)


def _pad_cols(w, width):
    return jnp.pad(w, ((0, 0), (0, width - w.shape[1])))


def _pad_rows(w, height):
    return jnp.pad(w, ((0, height - w.shape[0]), (0, 0)))


def _layout_a(w):
    w3 = 3 * A_WIDTH
    return jnp.concatenate([
        w[:, 0:w3], _pad_cols(w[:, w3:w3 + 64], LANES), _pad_cols(w[:, w3 + 64:w3 + 128], LANES),
        w[:, w3 + 128:A_IN]], axis=1)


def _layout_b(w):
    return jnp.concatenate([
        w[:, 0:B_QKV], _pad_cols(w[:, B_QKV:B_QKV + B_GATE_LORA], LANES),
        w[:, B_QKV + B_GATE_LORA:]], axis=1)


def kernel(x, ln_in_g, ln_in_b, w_in, a_mu, a_w0, a_w2, a_a0, a_a2, a_g2, a_k_k, a_k_a, a_r_k,
           a_gn_g, a_gn_b, b_conv, b_gk_w2, b_gk_b, b_norm_g, w_out, ln1_g, ln1_b, p_wq, p_keys,
           p_u, p_v, ln2_g, ln2_b):
    bsz, t, d = x.shape
    n = bsz * t
    row = lambda a: a.reshape(1, -1)

    h = x.reshape(n, d)
    for l in range(DEPTH):
        wa = _layout_a(w_in[l][:, :A_IN]).astype(BF16)
        wb = _layout_b(w_in[l][:, A_IN:]).astype(BF16)
        if l == 0:
            h, pa, pb = ln_proj(h, row(ln_in_g), row(ln_in_b), wa, wb)
        ya = rwkv7(
            pa.reshape(bsz, t, A_PAD), _layout_a(row(a_mu[l])), row(a_w0[l]),
            _pad_rows(a_w2[l], LANES), row(a_a0[l]), _pad_rows(a_a2[l], LANES), a_g2[l],
            row(a_k_k[l]), row(a_k_a[l]), row(a_r_k[l]), row(a_gn_g[l]), row(a_gn_b[l]))
        yb = gla(pb.reshape(bsz, t, B_PAD), b_conv[l], _pad_rows(b_gk_w2[l], LANES),
                 row(b_gk_b[l]), row(b_norm_g[l]))
        h1, h1b = out_ln(ya.reshape(n, A_WIDTH), yb.reshape(n, B_WIDTH), h,
                         w_out[l][:A_WIDTH].astype(BF16), w_out[l][A_WIDTH:].astype(BF16),
                         row(ln1_g[l]), row(ln1_b[l]))
        ws = fold_keys(p_keys[l].reshape(P_HEADS * 2, P_KEYS, P_HALF), p_wq[l])
        h = peer_fused(h1, h1b, ws, p_u[l].astype(BF16).T, p_v[l].astype(BF16),
                       row(ln2_g[l]), row(ln2_b[l]))
    return h.reshape(bsz, t, d)
```

```python
import functools
import math

import jax
import jax.numpy as jnp
from jax import lax
from jax.experimental import pallas as pl
from jax.experimental.pallas import tpu as pltpu

F32 = jnp.float32
BF16 = jnp.bfloat16
I32 = jnp.int32

D_MODEL = 1024
DEPTH = 1
CHUNK = 64
LN_EPS = 1e-5
ALPHA = (2.0 * DEPTH) ** 0.25

A_HEAD = 64
A_WIDTH = 512
A_HEADS = 8
A_GN_EPS = 64e-5
A_IN = 1792
A_PAD = 1920

B_HEADS = 4
B_DV = 128
B_DK = 64
B_WIDTH = 512
B_WIDTH_K = 256
B_QKV = 1024
B_GATE_LORA = 16
B_GATE_NORM = 16.0
B_CONV = 4
B_RMS_EPS = 1e-5
B_PAD = 1664

P_HEADS = 8
P_KEYS = 128
P_HALF = 128
P_TOPK = 16

LANES = 128
G_PITCH = 72

VMEM_LIMIT = 56 * 1024 * 1024


def _dot(a, b):
    return jnp.dot(a.astype(BF16), b.astype(BF16), preferred_element_type=F32)


def _dot_nt(a, b):
    return lax.dot_general(a.astype(BF16), b.astype(BF16), (((1,), (1,)), ((), ())),
                           preferred_element_type=F32)


def _dot_tn(a, b):
    return lax.dot_general(a.astype(BF16), b.astype(BF16), (((0,), (0,)), ((), ())),
                           preferred_element_type=F32)


def _split3(x):
    hi = x.astype(BF16)
    r1 = x - hi.astype(F32)
    mid = r1.astype(BF16)
    lo = (r1 - mid.astype(F32)).astype(BF16)
    return hi, mid, lo


def _dot_exact_rhs(a01, x):
    a = a01.astype(BF16)
    hi, mid, lo = _split3(x)
    out = jnp.dot(a, hi, preferred_element_type=F32)
    out += jnp.dot(a, mid, preferred_element_type=F32)
    out += jnp.dot(a, lo, preferred_element_type=F32)
    return out


def _dot_exact_lhs(x, b01):
    b = b01.astype(BF16)
    hi, mid, lo = _split3(x)
    out = jnp.dot(hi, b, preferred_element_type=F32)
    out += jnp.dot(mid, b, preferred_element_type=F32)
    out += jnp.dot(lo, b, preferred_element_type=F32)
    return out


def _sigmoid(x):
    return 1.0 / (1.0 + jnp.exp(-x))


def _softplus(x):
    return jnp.maximum(x, 0.0) + jnp.log(1.0 + jnp.exp(-jnp.abs(x)))


def _silu(x):
    return x * _sigmoid(x)


def _layer_norm(x, g, b):
    mu = jnp.mean(x, axis=-1, keepdims=True)
    xc = x - mu
    var = jnp.mean(xc * xc, axis=-1, keepdims=True)
    return xc * lax.rsqrt(var + LN_EPS) * g + b


def _block_diag_ones(n, seg):
    r = lax.broadcasted_iota(I32, (n, n), 0) // seg
    c = lax.broadcasted_iota(I32, (n, n), 1) // seg
    return jnp.where(r == c, 1.0, 0.0).astype(F32)


def _chunk_tril(n, seg):
    r = lax.broadcasted_iota(I32, (n, n), 0)
    c = lax.broadcasted_iota(I32, (n, n), 1)
    return jnp.where((r // seg == c // seg) & (c <= r), 1.0, 0.0).astype(F32)


def _ln_proj_kernel(x_ref, g_ref, b_ref, wa_ref, wb_ref, h_ref, pa_ref, pb_ref):
    h = _layer_norm(x_ref[...], g_ref[...], b_ref[...])
    h_ref[...] = h
    hb = h.astype(BF16)
    pa_ref[...] = jnp.dot(hb, wa_ref[...], preferred_element_type=F32)
    pb_ref[...] = jnp.dot(hb, wb_ref[...], preferred_element_type=F32)


def ln_proj(x2, g, b, wa, wb, tm=512):
    n = x2.shape[0]
    full = lambda i: (0, 0)
    return pl.pallas_call(
        _ln_proj_kernel,
        grid=(n // tm,),
        in_specs=[
            pl.BlockSpec((tm, D_MODEL), lambda i: (i, 0)),
            pl.BlockSpec((1, D_MODEL), full),
            pl.BlockSpec((1, D_MODEL), full),
            pl.BlockSpec((D_MODEL, A_PAD), full),
            pl.BlockSpec((D_MODEL, B_PAD), full),
        ],
        out_specs=[
            pl.BlockSpec((tm, D_MODEL), lambda i: (i, 0)),
            pl.BlockSpec((tm, A_PAD), lambda i: (i, 0)),
            pl.BlockSpec((tm, B_PAD), lambda i: (i, 0)),
        ],
        out_shape=[
            jax.ShapeDtypeStruct((n, D_MODEL), F32),
            jax.ShapeDtypeStruct((n, A_PAD), F32),
            jax.ShapeDtypeStruct((n, B_PAD), F32),
        ],
        compiler_params=pltpu.CompilerParams(
            dimension_semantics=("arbitrary",), vmem_limit_bytes=VMEM_LIMIT),
        name="ln_proj",
    )(x2, g, b, wa, wb)


def _rwkv_kernel(pa_ref, mu_ref, w0_ref, w2_ref, a0_ref, a2_ref, g2_ref, kkw_ref, kaw_ref,
                 rk_ref, gng_ref, gnb_ref, ya_ref,
                 carry_ref, z_ref, at_s, rt_s, bt_s, kt_s, bh_s, kh_s, v_s, el_s, y_s, *, tb):
    nck = tb // CHUNK
    w = A_WIDTH

    @pl.when(pl.program_id(1) == 0)
    def _():
        carry_ref[...] = jnp.zeros_like(carry_ref)
        z_ref[...] = jnp.zeros_like(z_ref)

    p = pa_ref[0]
    rolled = pltpu.roll(p, 1, axis=0)
    row = lax.broadcasted_iota(I32, p.shape, 0)
    prev = jnp.where(row == 0, carry_ref[0:1, :], rolled)
    carry_ref[0:1, :] = p[tb - 1:tb, :]
    ps = p + (prev - p) * mu_ref[...]

    r = ps[:, 0:w]
    k = ps[:, w:2 * w]
    v = ps[:, 2 * w:3 * w]
    wd = ps[:, 3 * w:3 * w + LANES]
    ad = ps[:, 3 * w + LANES:3 * w + 2 * LANES]
    gd = ps[:, 3 * w + 2 * LANES:3 * w + 3 * LANES]

    wlin = w0_ref[...] + _dot(jnp.tanh(wd), w2_ref[...])
    lw = -jnp.exp(-_softplus(-wlin) - 0.5)
    iclr = _sigmoid(a0_ref[...] + _dot(ad, a2_ref[...]))
    gate = _dot(_sigmoid(gd), g2_ref[...])

    bd = _block_diag_ones(w, A_HEAD)
    kkr = k * kkw_ref[...]
    ss = _dot_exact_lhs(kkr * kkr, bd)
    kk = kkr / jnp.maximum(jnp.sqrt(ss), 1e-12)
    kh = k * (1.0 + (iclr - 1.0) * kaw_ref[...])
    b = kk * iclr
    bonus = _dot_exact_lhs(r * kh * rk_ref[...], bd) * v

    lcum = _dot_exact_rhs(_chunk_tril(tb, CHUNK), lw)
    e_l = jnp.exp(lcum)
    e_nl = jnp.exp(-lcum)
    at_s[...] = -kk * jnp.exp(lcum - lw)
    rt_s[...] = r * e_l
    bt_s[...] = b * e_nl
    kt_s[...] = kh * e_nl
    v_s[...] = v
    for c in range(nck):
        rows = slice(c * CHUNK, (c + 1) * CHUNK)
        l_end = lcum[(c + 1) * CHUNK - 1:(c + 1) * CHUNK, :]
        e_lc = jnp.exp(l_end - lcum[rows, :])
        bh_s[rows, :] = b[rows, :] * e_lc
        kh_s[rows, :] = kh[rows, :] * e_lc
        el_s[c:c + 1, :] = jnp.exp(l_end)

    ri = lax.broadcasted_iota(I32, (2 * CHUNK, 2 * CHUNK), 0)
    ci = lax.broadcasted_iota(I32, (2 * CHUNK, 2 * CHUNK), 1) % CHUNK
    tri_mask = ci < jnp.where(ri < CHUNK, ri, ri - (CHUNK - 1))
    er = lax.broadcasted_iota(I32, (CHUNK, CHUNK), 0)
    ec = lax.broadcasted_iota(I32, (CHUNK, CHUNK), 1)
    eye_mask = er == ec
    eye = jnp.where(eye_mask, 1.0, 0.0).astype(F32)

    heads = range(A_HEADS)
    chains = [(c, h) for c in range(nck) for h in heads]
    rows = [slice(c * CHUNK, (c + 1) * CHUNK) for c in range(nck)]
    slabs = [{name: ref[rows[c], :] for name, ref in
              (("at", at_s), ("rt", rt_s), ("bt", bt_s), ("kt", kt_s), ("bh", bh_s), ("kh", kh_s), ("v", v_s))}
             for c in range(nck)]
    els = [el_s[c:c + 1, :] for c in range(nck)]

    def part(name, c, h):
        return slabs[c][name][:, h * A_HEAD:(h + 1) * A_HEAD]

    gms = [jnp.where(tri_mask,
                     _dot_nt(jnp.concatenate([part("at", c, h), part("rt", c, h)], axis=0),
                             jnp.concatenate([part("bt", c, h), part("kt", c, h)], axis=0)), 0.0)
           for c, h in chains]
    mvs = [_dot(gm[:, CHUNK:2 * CHUNK], part("v", c, h)) for gm, (c, h) in zip(gms, chains)]
    khv = [_dot_tn(part("kh", c, h), part("v", c, h)) for c, h in chains]
    pws = [gm[0:CHUNK, 0:CHUNK] for gm in gms]
    tms = [eye + pw for pw in pws]
    for _ in range(5):
        pws = [_dot(pw, pw) for pw in pws]
        tms = [tm + _dot(tm, pw) for tm, pw in zip(tms, pws)]
    p12s = [_dot(tm, jnp.concatenate([mv[0:CHUNK], part("at", c, h)], axis=1))
            for tm, mv, (c, h) in zip(tms, mvs, chains)]
    yqs = [jnp.concatenate([mv[CHUNK:2 * CHUNK], part("rt", c, h)], axis=1)
           + _dot(gm[CHUNK:2 * CHUNK, 0:CHUNK], p12)
           for mv, gm, p12, (c, h) in zip(mvs, gms, p12s, chains)]
    zws = [jnp.concatenate(
               [kv, jnp.where(eye_mask,
                              jnp.broadcast_to(els[c][:, h * A_HEAD:(h + 1) * A_HEAD], (CHUNK, CHUNK)), 0.0)],
               axis=1)
           + _dot_tn(part("bh", c, h), p12)
           for kv, p12, (c, h) in zip(khv, p12s, chains)]

    zs = [z_ref[h] for h in heads]
    for c in range(nck):
        ys = []
        for h in heads:
            yq, zw = yqs[c * A_HEADS + h], zws[c * A_HEADS + h]
            ys.append(yq[:, 0:CHUNK] + _dot(yq[:, CHUNK:2 * CHUNK], zs[h]))
            zs[h] = zw[:, 0:CHUNK] + _dot(zw[:, CHUNK:2 * CHUNK], zs[h])
        y_s[rows[c], :] = jnp.concatenate(ys, axis=1)
    for h in heads:
        z_ref[h] = zs[h]

    y = y_s[...]
    inv_n = 1.0 / A_HEAD
    mean = _dot_exact_lhs(y, bd) * inv_n
    yc = y - mean
    var = _dot_exact_lhs(yc * yc, bd) * inv_n
    yn = yc * lax.rsqrt(var + A_GN_EPS) * gng_ref[...] + gnb_ref[...]
    ya_ref[0] = (yn + bonus) * gate


def rwkv7(pa3, mu, w0, w2, a0, a2, g2, kkw, kaw, rk, gng, gnb, tb=256):
    bsz, t, _ = pa3.shape
    full = lambda bi, ti: (0, 0)
    vec = pl.BlockSpec((1, A_WIDTH), full)
    slab = pltpu.VMEM((tb, A_WIDTH), F32)
    return pl.pallas_call(
        functools.partial(_rwkv_kernel, tb=tb),
        grid=(bsz, t // tb),
        in_specs=[
            pl.BlockSpec((1, tb, A_PAD), lambda bi, ti: (bi, ti, 0)),
            pl.BlockSpec((1, A_PAD), full),
            vec,
            pl.BlockSpec((LANES, A_WIDTH), full),
            vec,
            pl.BlockSpec((LANES, A_WIDTH), full),
            pl.BlockSpec((LANES, A_WIDTH), full),
            vec, vec, vec, vec, vec,
        ],
        out_specs=pl.BlockSpec((1, tb, A_WIDTH), lambda bi, ti: (bi, ti, 0)),
        out_shape=jax.ShapeDtypeStruct((bsz, t, A_WIDTH), F32),
        scratch_shapes=[
            pltpu.VMEM((8, A_PAD), F32),
            pltpu.VMEM((A_HEADS, A_HEAD, A_HEAD), F32),
            slab, slab, slab, slab, slab, slab, slab,
            pltpu.VMEM((8, A_WIDTH), F32),
            slab,
        ],
        compiler_params=pltpu.CompilerParams(
            dimension_semantics=("arbitrary", "arbitrary"), vmem_limit_bytes=VMEM_LIMIT),
        name="rwkv7",
    )(pa3, mu, w0, w2, a0, a2, g2, kkw, kaw, rk, gng, gnb)


def _gla_kernel(pb_ref, cw_ref, gw_ref, gb_ref, ng_ref, yb_ref,
                xpad_ref, s_ref, qt_s, kt_s, kh_s, v_s, gk_s, o_s, *, tb):
    nck = tb // CHUNK

    @pl.when(pl.program_id(1) == 0)
    def _():
        xpad_ref[0:8, :] = jnp.zeros((8, B_QKV), F32)
        s_ref[...] = jnp.zeros_like(s_ref)

    p = pb_ref[0]
    xpad_ref[8:8 + tb, :] = p[:, 0:B_QKV]
    conv = cw_ref[B_CONV - 1:B_CONV, :] * p[:, 0:B_QKV]
    for j in range(B_CONV - 1):
        sh = B_CONV - 1 - j
        conv = conv + cw_ref[j:j + 1, :] * xpad_ref[8 - sh:8 - sh + tb, :]
    xpad_ref[0:8, :] = p[tb - 8:tb, 0:B_QKV]
    qkv = _silu(conv)
    q = qkv[:, 0:B_WIDTH_K] * (B_DK ** -0.5)
    k = qkv[:, B_WIDTH_K:2 * B_WIDTH_K]
    v_s[...] = qkv[:, 2 * B_WIDTH_K:]
    gkd = p[:, B_QKV:B_QKV + LANES]
    gz = _dot(gkd, gw_ref[...]) + gb_ref[...]
    gk = -_softplus(-gz) * (1.0 / B_GATE_NORM)
    gk_s[...] = gk
    gcum = _dot_exact_rhs(_chunk_tril(tb, CHUNK), gk)
    qt_s[...] = q * jnp.exp(gcum)
    kt_s[...] = k * jnp.exp(-gcum)
    for c in range(nck):
        rows = slice(c * CHUNK, (c + 1) * CHUNK)
        g_end = gcum[(c + 1) * CHUNK - 1:(c + 1) * CHUNK, :]
        kh_s[rows, :] = k[rows, :] * jnp.exp(g_end - gcum[rows, :])

    ri = lax.broadcasted_iota(I32, (CHUNK, CHUNK), 0)
    ci = lax.broadcasted_iota(I32, (CHUNK, CHUNK), 1)
    causal = ci <= ri
    ones_cv = jnp.ones((CHUNK, B_DV), F32)

    heads = range(B_HEADS)
    chains = [(c, h) for c in range(nck) for h in heads]
    rows = [slice(c * CHUNK, (c + 1) * CHUNK) for c in range(nck)]
    slabs = [{name: ref[rows[c], :] for name, ref in
              (("qt", qt_s), ("kt", kt_s), ("kh", kh_s), ("v", v_s), ("gk", gk_s))} for c in range(nck)]

    def key_part(name, c, h):
        return slabs[c][name][:, h * B_DK:(h + 1) * B_DK]

    def val_part(c, h):
        return slabs[c]["v"][:, h * B_DV:(h + 1) * B_DV]

    ob = ones_cv.astype(BF16)
    dn = (((0,), (0,)), ((), ()))
    intra = [_dot(jnp.where(causal, _dot_nt(key_part("qt", c, h), key_part("kt", c, h)), 0.0), val_part(c, h))
             for c, h in chains]
    kvs = [_dot_tn(key_part("kh", c, h), val_part(c, h)) for c, h in chains]
    decays = []
    for c, h in chains:
        hi, mid, lo = _split3(key_part("gk", c, h))
        gcol = (lax.dot_general(hi, ob, dn, preferred_element_type=F32)
                + lax.dot_general(mid, ob, dn, preferred_element_type=F32)
                + lax.dot_general(lo, ob, dn, preferred_element_type=F32))
        decays.append(jnp.exp(gcol))
    ss = [s_ref[h] for h in heads]
    for c in range(nck):
        os_ = []
        for h in heads:
            n = c * B_HEADS + h
            os_.append(intra[n] + _dot(key_part("qt", c, h), ss[h]))
            ss[h] = ss[h] * decays[n] + kvs[n]
        o_s[rows[c], :] = jnp.concatenate(os_, axis=1)
    for h in heads:
        s_ref[h] = ss[h]

    g = p[:, B_QKV + LANES:]
    outs = []
    for h in range(B_HEADS):
        sv = slice(h * B_DV, (h + 1) * B_DV)
        o = o_s[:, sv]
        o = o * lax.rsqrt(jnp.mean(o * o, axis=-1, keepdims=True) + B_RMS_EPS) * ng_ref[...]
        outs.append(o * _silu(g[:, sv]))
    yb_ref[0] = jnp.concatenate(outs, axis=1)


def gla(pb3, cw, gw, gb, ng, tb=256):
    bsz, t, _ = pb3.shape
    full = lambda bi, ti: (0, 0)
    return pl.pallas_call(
        functools.partial(_gla_kernel, tb=tb),
        grid=(bsz, t // tb),
        in_specs=[
            pl.BlockSpec((1, tb, B_PAD), lambda bi, ti: (bi, ti, 0)),
            pl.BlockSpec((B_CONV, B_QKV), full),
            pl.BlockSpec((LANES, B_WIDTH_K), full),
            pl.BlockSpec((1, B_WIDTH_K), full),
            pl.BlockSpec((1, B_DV), full),
        ],
        out_specs=pl.BlockSpec((1, tb, B_WIDTH), lambda bi, ti: (bi, ti, 0)),
        out_shape=jax.ShapeDtypeStruct((bsz, t, B_WIDTH), F32),
        scratch_shapes=[
            pltpu.VMEM((tb + 8, B_QKV), F32),
            pltpu.VMEM((B_HEADS, B_DK, B_DV), F32),
            pltpu.VMEM((tb, B_WIDTH_K), F32),
            pltpu.VMEM((tb, B_WIDTH_K), F32),
            pltpu.VMEM((tb, B_WIDTH_K), F32),
            pltpu.VMEM((tb, B_WIDTH), F32),
            pltpu.VMEM((tb, B_WIDTH_K), F32),
            pltpu.VMEM((tb, B_WIDTH), F32),
        ],
        compiler_params=pltpu.CompilerParams(
            dimension_semantics=("arbitrary", "arbitrary"), vmem_limit_bytes=VMEM_LIMIT),
        name="gla",
    )(pb3, cw, gw, gb, ng)


def _out_ln_kernel(ya_ref, yb_ref, h_ref, wa_ref, wb_ref, g_ref, b_ref, h1_ref, h1b_ref):
    y = _dot(ya_ref[...], wa_ref[...]) + _dot(yb_ref[...], wb_ref[...])
    h1 = _layer_norm(ALPHA * h_ref[...] + y, g_ref[...], b_ref[...])
    h1_ref[...] = h1
    h1b_ref[...] = h1.astype(BF16)


def out_ln(ya, yb, h, wa, wb, g, b, tm=512):
    n = h.shape[0]
    full = lambda i: (0, 0)
    return pl.pallas_call(
        _out_ln_kernel,
        grid=(n // tm,),
        in_specs=[
            pl.BlockSpec((tm, A_WIDTH), lambda i: (i, 0)),
            pl.BlockSpec((tm, B_WIDTH), lambda i: (i, 0)),
            pl.BlockSpec((tm, D_MODEL), lambda i: (i, 0)),
            pl.BlockSpec((A_WIDTH, D_MODEL), full),
            pl.BlockSpec((B_WIDTH, D_MODEL), full),
            pl.BlockSpec((1, D_MODEL), full),
            pl.BlockSpec((1, D_MODEL), full),
        ],
        out_specs=[
            pl.BlockSpec((tm, D_MODEL), lambda i: (i, 0)),
            pl.BlockSpec((tm, D_MODEL), lambda i: (i, 0)),
        ],
        out_shape=[
            jax.ShapeDtypeStruct((n, D_MODEL), F32),
            jax.ShapeDtypeStruct((n, D_MODEL), BF16),
        ],
        compiler_params=pltpu.CompilerParams(
            dimension_semantics=("arbitrary",), vmem_limit_bytes=VMEM_LIMIT),
        name="out_ln",
    )(ya, yb, h, wa, wb, g, b)


def _fold_keys_kernel(keys_ref, wq_ref, out_ref):
    kh, km, kl = _split3(keys_ref[0])
    wh, wm, wl = _split3(wq_ref[...])
    dn = (((1,), (1,)), ((), ()))
    acc = lax.dot_general(kh, wh, dn, preferred_element_type=F32)
    for a, b in ((kh, wm), (km, wh), (km, wm), (kh, wl), (kl, wh)):
        acc += lax.dot_general(a, b, dn, preferred_element_type=F32)
    out_ref[...] = acc.astype(BF16)


def fold_keys(keys3, wq):
    nhp = keys3.shape[0]
    return pl.pallas_call(
        _fold_keys_kernel,
        grid=(nhp,),
        in_specs=[
            pl.BlockSpec((1, P_KEYS, P_HALF), lambda i: (i, 0, 0)),
            pl.BlockSpec((D_MODEL, P_HALF), lambda i: (0, i)),
        ],
        out_specs=pl.BlockSpec((P_KEYS, D_MODEL), lambda i: (i, 0)),
        out_shape=jax.ShapeDtypeStruct((nhp * P_KEYS, D_MODEL), BF16),
        compiler_params=pltpu.CompilerParams(dimension_semantics=("arbitrary",)),
        name="fold_keys",
    )(keys3, wq)


def _cand_index(tt):
    r = lax.broadcasted_iota(I32, (72, tt), 0)
    c = jnp.where(r < 24, r, jnp.where(r < 32, r + 8, r + 16))
    c = jnp.where(r >= 40, (r - 40) * P_TOPK, c)
    c = jnp.where(r >= 56, (r - 56) * P_TOPK + 1, c)
    c = jnp.where(r >= 64, (r - 64) * P_TOPK + 2, c)
    pos = jnp.where(r >= 64, r - 64, jnp.where(r >= 56, r - 56, jnp.where(r >= 40, r - 40, 4)))
    return c.astype(F32), pos < 4


def _pick16(table, sel):
    out = jnp.zeros_like(table)
    for i in range(P_TOPK):
        out = jnp.where(sel == float(i), table[i:i + 1, :], out)
    return out


def _top16_steps(vals, idx):
    out_v, out_i = [], []
    for _ in range(P_TOPK):
        m = jnp.max(vals, axis=0, keepdims=True)
        i = jnp.min(jnp.where(vals == m, idx, float(2 ** 20)), axis=0, keepdims=True)
        out_v.append(m)
        out_i.append(i)
        vals = jnp.where(idx == i, -jnp.inf, vals)
        yield
    return jnp.concatenate(out_v, axis=0), jnp.concatenate(out_i, axis=0)


def _head_topk_steps(h, half, s_scr, e1s, e2s, gts, key_idx, cand_idx, cand_dup):
    r0 = pl.multiple_of(h * 2 * P_KEYS, 2 * P_KEYS)
    sv1, si1 = yield from _top16_steps(s_scr[half, pl.ds(r0, P_KEYS), :], key_idx)
    sv2, si2 = yield from _top16_steps(s_scr[half, pl.ds(r0 + P_KEYS, P_KEYS), :], key_idx)
    cand = jnp.concatenate(
        [sv1[0:1, :] + sv2]
        + [sv1[i:i + 1, :] + sv2[0:8, :] for i in (1, 2, 3)]
        + [sv1 + sv2[0:1, :]]
        + [sv1[0:8, :] + sv2[j:j + 1, :] for j in (1, 2)], axis=0)
    cv, cidx = yield from _top16_steps(jnp.where(cand_dup, -jnp.inf, cand), cand_idx)
    qi = jnp.floor(cidx * (1.0 / P_TOPK))
    ri = cidx - qi * P_TOPK
    ex = jnp.exp(cv - cv[0:1, :])
    o0 = pl.multiple_of(h * P_TOPK, P_TOPK)
    e1s[half, pl.ds(o0, P_TOPK), :] = _pick16(si1, qi)
    e2s[half, pl.ds(o0, P_TOPK), :] = _pick16(si2, ri)
    gts[half, pl.ds(o0, P_TOPK), :] = ex / jnp.sum(ex, axis=0, keepdims=True)


def _head_topk(*args, **kwargs):
    for _ in _head_topk_steps(*args, **kwargs):
        pass


def _gelu_x2(x):
    return x * (1.0 + lax.erf(x * (1.0 / math.sqrt(2.0))))


def _peer_fused_kernel(h1_ref, h1b_ref, h1bn_ref, ws_ref, ulo_ref, uhi_ref, vlo_ref, vhi_ref,
                       g_ref, b_ref, out_ref,
                       gs_ref, acc_ref, s_scr, e1s, e2s, gts, tm1, tm2, tmg, *, tt, ngrp):
    i = pl.program_id(0)
    j = pl.program_id(1)
    nsel = P_HEADS * P_TOPK
    half_keys = P_KEYS // 2
    th = tt // 2
    key_idx = lax.broadcasted_iota(I32, (P_KEYS, th), 0).astype(F32)
    cand_idx, cand_dup = _cand_index(th)
    topk = functools.partial(_head_topk, s_scr=s_scr, e1s=e1s, e2s=e2s, gts=gts,
                             key_idx=key_idx, cand_idx=cand_idx, cand_dup=cand_dup)

    def scores(hb_ref):
        for half in range(2):
            s_scr[half] = lax.dot_general(ws_ref[...], hb_ref[half * th:(half + 1) * th, :],
                                          (((1,), (1,)), ((), ())), preferred_element_type=F32)

    @pl.when((i == 0) & (j == 0))
    def _():
        scores(h1b_ref)

        def body(t, carry):
            topk(t // 2, t % 2)
            return carry

        lax.fori_loop(0, 2 * P_HEADS, body, 0)

    @pl.when(j == 0)
    def _():
        acc_ref[...] = jnp.zeros_like(acc_ref)
        for half in range(2):
            rows = slice(half * th, (half + 1) * th)
            tm1[rows, :] = e1s[half].T
            tm2[rows, :] = e2s[half].T
            tmg[rows, :] = gts[half].T
        sub = lax.broadcasted_iota(I32, (P_KEYS, nsel), 0).astype(F32)
        zero = jnp.zeros((P_KEYS, nsel), F32)

        def pair_body(m, carry):
            n0 = 2 * m
            onehots = []
            gated = []
            for q in range(2):
                e1 = tm1[pl.ds(n0 + q, 1), :]
                e2 = tm2[pl.ds(n0 + q, 1), :]
                gt = 0.5 * tmg[pl.ds(n0 + q, 1), :]
                gated.append(jnp.where(sub == e1, gt, 0.0))
                onehots.append(jnp.where(sub == e2, 1.0, 0.0))
            a_cat = jnp.concatenate(gated, axis=1).astype(BF16)
            b_bd = jnp.concatenate(
                [jnp.concatenate([onehots[0], zero], axis=1),
                 jnp.concatenate([zero, onehots[1]], axis=1)], axis=0).astype(BF16)
            gn = lax.dot_general(a_cat, b_bd, (((1,), (1,)), ((), ())),
                                 preferred_element_type=F32)
            bits = pltpu.bitcast(gn, I32) + 0x8000
            packed = (bits[half_keys:, :] & -65536) | lax.shift_right_logical(bits[:half_keys, :], 16)
            for q in range(2):
                r0 = pl.multiple_of((n0 + q) * G_PITCH, 8)
                gs_ref[pl.ds(r0, half_keys), :] = packed[:, q * P_KEYS:(q + 1) * P_KEYS]
            return carry

        lax.fori_loop(0, tt // 2, pair_body, 0, unroll=16)
        scores(h1bn_ref)

    nper = 2 * P_HEADS // (half_keys // ngrp)
    units = [_head_topk_steps((j * nper + u) // 2, (j * nper + u) % 2, s_scr, e1s, e2s, gts,
                              key_idx, cand_idx, cand_dup) for u in range(nper)]
    msplit = 2
    npiece = msplit * (2 * (ngrp // 2) + 2)
    per_piece = -(-3 * P_TOPK * nper // npiece)
    tm = tt // msplit

    def advance(count):
        for _ in range(count):
            while units:
                try:
                    next(units[0])
                    break
                except StopIteration:
                    units.pop(0)

    sub = 2 * P_KEYS
    for mi in range(msplit):
        trow = slice(mi * tm, (mi + 1) * tm)
        x = h1b_ref[trow, :]
        h_lo, h_hi = [], []
        for k in range(ngrp // 2):
            words = [gs_ref[pl.ds((mi * tm) * G_PITCH + j * ngrp + 2 * k + q, tm, stride=G_PITCH), :]
                     for q in range(2)]
            cols = slice(k * sub, (k + 1) * sub)
            g_lo = jnp.concatenate([pltpu.bitcast(w << 16, F32) for w in words], axis=1)
            s_lo = jnp.dot(x, ulo_ref[:, cols], preferred_element_type=F32)
            h_lo.append((g_lo * _gelu_x2(s_lo)).astype(BF16))
            advance(per_piece)
            g_hi = jnp.concatenate([pltpu.bitcast(w & -65536, F32) for w in words], axis=1)
            s_hi = jnp.dot(x, uhi_ref[:, cols], preferred_element_type=F32)
            h_hi.append((g_hi * _gelu_x2(s_hi)).astype(BF16))
            advance(per_piece)
        acc = acc_ref[trow, :]
        acc = acc + jnp.dot(jnp.concatenate(h_lo, axis=1), vlo_ref[...], preferred_element_type=F32)
        advance(per_piece)
        acc = acc + jnp.dot(jnp.concatenate(h_hi, axis=1), vhi_ref[...], preferred_element_type=F32)
        advance(per_piece)
        acc_ref[trow, :] = acc
    advance(4 * P_TOPK * nper)

    @pl.when(j == pl.num_programs(1) - 1)
    def _():
        out_ref[...] = _layer_norm(ALPHA * h1_ref[...] + acc_ref[...], g_ref[...], b_ref[...])


def peer_fused(h1, h1b, ws, ubt, vb, g, b, tt=512, ngrp=4):
    n = h1.shape[0]
    nsel = P_HEADS * P_TOPK
    nrow = P_HEADS * 2 * P_KEYS
    ntile = n // tt
    nstep = P_KEYS // 2 // ngrp
    assert (2 * P_HEADS) % nstep == 0, "each step runs a whole number of (head, token half) top-k units"
    rows = ngrp * P_KEYS
    th = tt // 2
    tok = lambda i, j: (i, 0)
    full = lambda i, j: (0, 0)
    once = pl.Buffered(1)
    sel = pltpu.VMEM((2, nsel, th), F32)
    tmaj = pltpu.VMEM((tt, nsel), F32)
    return pl.pallas_call(
        functools.partial(_peer_fused_kernel, tt=tt, ngrp=ngrp),
        grid=(ntile, nstep),
        in_specs=[
            pl.BlockSpec((tt, D_MODEL), tok, pipeline_mode=once),
            pl.BlockSpec((tt, D_MODEL), tok, pipeline_mode=once),
            pl.BlockSpec((tt, D_MODEL), lambda i, j: (jnp.minimum(i + 1, ntile - 1), 0), pipeline_mode=once),
            pl.BlockSpec((nrow, D_MODEL), full, pipeline_mode=once),
            pl.BlockSpec((D_MODEL, rows), lambda i, j: (0, j)),
            pl.BlockSpec((D_MODEL, rows), lambda i, j: (0, j + nstep)),
            pl.BlockSpec((rows, D_MODEL), lambda i, j: (j, 0)),
            pl.BlockSpec((rows, D_MODEL), lambda i, j: (j + nstep, 0)),
            pl.BlockSpec((1, D_MODEL), full),
            pl.BlockSpec((1, D_MODEL), full),
        ],
        out_specs=pl.BlockSpec((tt, D_MODEL), tok),
        out_shape=jax.ShapeDtypeStruct((n, D_MODEL), F32),
        scratch_shapes=[
            pltpu.VMEM((tt * G_PITCH, P_KEYS), I32),
            pltpu.VMEM((tt, D_MODEL), F32),
            pltpu.VMEM((2, nrow, th), F32),
            sel, sel, sel,
            tmaj, tmaj, tmaj,
        ],
        compiler_params=pltpu.CompilerParams(
            dimension_semantics=("arbitrary", "arbitrary"), vmem_limit_bytes=VMEM_LIMIT),
        name="peer_fused",
    )(h1, h1b, h1b, ws, ubt, ubt, vb, vb, g, b)


def _pad_cols(w, width):
    return jnp.pad(w, ((0, 0), (0, width - w.shape[1])))


def _pad_rows(w, height):
    return jnp.pad(w, ((0, height - w.shape[0]), (0, 0)))


def _layout_a(w):
    w3 = 3 * A_WIDTH
    return jnp.concatenate([
        w[:, 0:w3], _pad_cols(w[:, w3:w3 + 64], LANES), _pad_cols(w[:, w3 + 64:w3 + 128], LANES),
        w[:, w3 + 128:A_IN]], axis=1)


def _layout_b(w):
    return jnp.concatenate([
        w[:, 0:B_QKV], _pad_cols(w[:, B_QKV:B_QKV + B_GATE_LORA], LANES),
        w[:, B_QKV + B_GATE_LORA:]], axis=1)


def kernel(x, ln_in_g, ln_in_b, w_in, a_mu, a_w0, a_w2, a_a0, a_a2, a_g2, a_k_k, a_k_a, a_r_k,
           a_gn_g, a_gn_b, b_conv, b_gk_w2, b_gk_b, b_norm_g, w_out, ln1_g, ln1_b, p_wq, p_keys,
           p_u, p_v, ln2_g, ln2_b):
    bsz, t, d = x.shape
    n = bsz * t
    row = lambda a: a.reshape(1, -1)

    assert DEPTH == 1 and w_in.shape[0] == 1, "the entry LayerNorm is fused with the single layer's projection"
    l = 0
    wa = _layout_a(w_in[l][:, :A_IN]).astype(BF16)
    wb = _layout_b(w_in[l][:, A_IN:]).astype(BF16)
    h, pa, pb = ln_proj(x.reshape(n, d), row(ln_in_g), row(ln_in_b), wa, wb)
    ya = rwkv7(
        pa.reshape(bsz, t, A_PAD), _layout_a(row(a_mu[l])), row(a_w0[l]),
        _pad_rows(a_w2[l], LANES), row(a_a0[l]), _pad_rows(a_a2[l], LANES), a_g2[l],
        row(a_k_k[l]), row(a_k_a[l]), row(a_r_k[l]), row(a_gn_g[l]), row(a_gn_b[l]))
    yb = gla(pb.reshape(bsz, t, B_PAD), b_conv[l], _pad_rows(b_gk_w2[l], LANES),
             row(b_gk_b[l]), row(b_norm_g[l]))
    h1, h1b = out_ln(ya.reshape(n, A_WIDTH), yb.reshape(n, B_WIDTH), h,
                     w_out[l][:A_WIDTH].astype(BF16), w_out[l][A_WIDTH:].astype(BF16),
                     row(ln1_g[l]), row(ln1_b[l]))
    ws = fold_keys(p_keys[l].reshape(P_HEADS * 2, P_KEYS, P_HALF), p_wq[l])
    out = peer_fused(h1, h1b, ws, p_u[l].astype(BF16).T, p_v[l].astype(BF16),
                     row(ln2_g[l]), row(ln2_b[l]))
    return out.reshape(bsz, t, d)
```

```python
import functools
import math

import jax
import jax.numpy as jnp
from jax import lax
from jax.experimental import pallas as pl
from jax.experimental.pallas import tpu as pltpu

F32 = jnp.float32
BF16 = jnp.bfloat16
I32 = jnp.int32

D_MODEL = 1024
DEPTH = 1
CHUNK = 64
LN_EPS = 1e-5
ALPHA = (2.0 * DEPTH) ** 0.25

A_HEAD = 64
A_WIDTH = 512
A_HEADS = 8
A_GN_EPS = 64e-5
A_IN = 1792
A_PAD = 1920

B_HEADS = 4
B_DV = 128
B_DK = 64
B_WIDTH = 512
B_WIDTH_K = 256
B_QKV = 1024
B_GATE_LORA = 16
B_GATE_NORM = 16.0
B_CONV = 4
B_RMS_EPS = 1e-5
B_PAD = 1664

P_HEADS = 8
P_KEYS = 128
P_HALF = 128
P_TOPK = 16

PEER_NGRP = 4

LANES = 128
G_PITCH = 72

VMEM_LIMIT = 56 * 1024 * 1024


def _dot(a, b):
    return jnp.dot(a.astype(BF16), b.astype(BF16), preferred_element_type=F32)


def _dot_nt(a, b):
    return lax.dot_general(a.astype(BF16), b.astype(BF16), (((1,), (1,)), ((), ())),
                           preferred_element_type=F32)


def _dot_tn(a, b):
    return lax.dot_general(a.astype(BF16), b.astype(BF16), (((0,), (0,)), ((), ())),
                           preferred_element_type=F32)


def _split3(x):
    hi = x.astype(BF16)
    r1 = x - hi.astype(F32)
    mid = r1.astype(BF16)
    lo = (r1 - mid.astype(F32)).astype(BF16)
    return hi, mid, lo


def _dot_exact_rhs(a01, x):
    a = a01.astype(BF16)
    hi, mid, lo = _split3(x)
    out = jnp.dot(a, hi, preferred_element_type=F32)
    out += jnp.dot(a, mid, preferred_element_type=F32)
    out += jnp.dot(a, lo, preferred_element_type=F32)
    return out


def _dot_exact_lhs(x, b01):
    b = b01.astype(BF16)
    hi, mid, lo = _split3(x)
    out = jnp.dot(hi, b, preferred_element_type=F32)
    out += jnp.dot(mid, b, preferred_element_type=F32)
    out += jnp.dot(lo, b, preferred_element_type=F32)
    return out


def _sigmoid(x):
    return 1.0 / (1.0 + jnp.exp(-x))


def _softplus(x):
    return jnp.maximum(x, 0.0) + jnp.log(1.0 + jnp.exp(-jnp.abs(x)))


def _silu(x):
    return x * _sigmoid(x)


def _layer_norm(x, g, b):
    mu = jnp.mean(x, axis=-1, keepdims=True)
    xc = x - mu
    var = jnp.mean(xc * xc, axis=-1, keepdims=True)
    return xc * lax.rsqrt(var + LN_EPS) * g + b


def _block_diag_ones(n, seg):
    r = lax.broadcasted_iota(I32, (n, n), 0) // seg
    c = lax.broadcasted_iota(I32, (n, n), 1) // seg
    return jnp.where(r == c, 1.0, 0.0).astype(F32)


def _chunk_tril(n, seg):
    r = lax.broadcasted_iota(I32, (n, n), 0)
    c = lax.broadcasted_iota(I32, (n, n), 1)
    return jnp.where((r // seg == c // seg) & (c <= r), 1.0, 0.0).astype(F32)


def _ln_proj_kernel(x_ref, g_ref, b_ref, wa_ref, wb_ref, h_ref, pa_ref, pb_ref):
    h = _layer_norm(x_ref[...], g_ref[...], b_ref[...])
    h_ref[...] = h
    hb = h.astype(BF16)
    pa_ref[...] = jnp.dot(hb, wa_ref[...], preferred_element_type=F32)
    pb_ref[...] = jnp.dot(hb, wb_ref[...], preferred_element_type=F32)


def ln_proj(x2, g, b, wa, wb, tm=512):
    n = x2.shape[0]
    full = lambda i: (0, 0)
    return pl.pallas_call(
        _ln_proj_kernel,
        grid=(n // tm,),
        in_specs=[
            pl.BlockSpec((tm, D_MODEL), lambda i: (i, 0)),
            pl.BlockSpec((1, D_MODEL), full),
            pl.BlockSpec((1, D_MODEL), full),
            pl.BlockSpec((D_MODEL, A_PAD), full),
            pl.BlockSpec((D_MODEL, B_PAD), full),
        ],
        out_specs=[
            pl.BlockSpec((tm, D_MODEL), lambda i: (i, 0)),
            pl.BlockSpec((tm, A_PAD), lambda i: (i, 0)),
            pl.BlockSpec((tm, B_PAD), lambda i: (i, 0)),
        ],
        out_shape=[
            jax.ShapeDtypeStruct((n, D_MODEL), F32),
            jax.ShapeDtypeStruct((n, A_PAD), F32),
            jax.ShapeDtypeStruct((n, B_PAD), F32),
        ],
        compiler_params=pltpu.CompilerParams(
            dimension_semantics=("arbitrary",), vmem_limit_bytes=VMEM_LIMIT),
        name="ln_proj",
    )(x2, g, b, wa, wb)


def _rwkv_kernel(pa_ref, mu_ref, w0_ref, w2_ref, a0_ref, a2_ref, g2_ref, kkw_ref, kaw_ref,
                 rk_ref, gng_ref, gnb_ref, ya_ref,
                 carry_ref, z_ref, at_s, rt_s, bt_s, kt_s, bh_s, kh_s, v_s, el_s, y_s, *, tb):
    nck = tb // CHUNK
    w = A_WIDTH

    @pl.when(pl.program_id(1) == 0)
    def _():
        carry_ref[...] = jnp.zeros_like(carry_ref)
        z_ref[...] = jnp.zeros_like(z_ref)

    p = pa_ref[0]
    rolled = pltpu.roll(p, 1, axis=0)
    row = lax.broadcasted_iota(I32, p.shape, 0)
    prev = jnp.where(row == 0, carry_ref[0:1, :], rolled)
    carry_ref[0:1, :] = p[tb - 1:tb, :]
    ps = p + (prev - p) * mu_ref[...]

    r = ps[:, 0:w]
    k = ps[:, w:2 * w]
    v = ps[:, 2 * w:3 * w]
    wd = ps[:, 3 * w:3 * w + LANES]
    ad = ps[:, 3 * w + LANES:3 * w + 2 * LANES]
    gd = ps[:, 3 * w + 2 * LANES:3 * w + 3 * LANES]

    wlin = w0_ref[...] + _dot(jnp.tanh(wd), w2_ref[...])
    lw = -jnp.exp(-_softplus(-wlin) - 0.5)
    iclr = _sigmoid(a0_ref[...] + _dot(ad, a2_ref[...]))
    gate = _dot(_sigmoid(gd), g2_ref[...])

    bd = _block_diag_ones(w, A_HEAD)
    kkr = k * kkw_ref[...]
    ss = _dot_exact_lhs(kkr * kkr, bd)
    kk = kkr / jnp.maximum(jnp.sqrt(ss), 1e-12)
    kh = k * (1.0 + (iclr - 1.0) * kaw_ref[...])
    b = kk * iclr
    bonus = _dot_exact_lhs(r * kh * rk_ref[...], bd) * v

    lcum = _dot_exact_rhs(_chunk_tril(tb, CHUNK), lw)
    e_l = jnp.exp(lcum)
    e_nl = jnp.exp(-lcum)
    at_s[...] = -kk * jnp.exp(lcum - lw)
    rt_s[...] = r * e_l
    bt_s[...] = b * e_nl
    kt_s[...] = kh * e_nl
    v_s[...] = v
    for c in range(nck):
        rows = slice(c * CHUNK, (c + 1) * CHUNK)
        l_end = lcum[(c + 1) * CHUNK - 1:(c + 1) * CHUNK, :]
        e_lc = jnp.exp(l_end - lcum[rows, :])
        bh_s[rows, :] = b[rows, :] * e_lc
        kh_s[rows, :] = kh[rows, :] * e_lc
        el_s[c:c + 1, :] = jnp.exp(l_end)

    ri = lax.broadcasted_iota(I32, (2 * CHUNK, 2 * CHUNK), 0)
    ci = lax.broadcasted_iota(I32, (2 * CHUNK, 2 * CHUNK), 1) % CHUNK
    tri_mask = ci < jnp.where(ri < CHUNK, ri, ri - (CHUNK - 1))
    er = lax.broadcasted_iota(I32, (CHUNK, CHUNK), 0)
    ec = lax.broadcasted_iota(I32, (CHUNK, CHUNK), 1)
    eye_mask = er == ec
    eye = jnp.where(eye_mask, 1.0, 0.0).astype(F32)

    heads = range(A_HEADS)
    chains = [(c, h) for c in range(nck) for h in heads]
    rows = [slice(c * CHUNK, (c + 1) * CHUNK) for c in range(nck)]
    slabs = [{name: ref[rows[c], :] for name, ref in
              (("at", at_s), ("rt", rt_s), ("bt", bt_s), ("kt", kt_s), ("bh", bh_s), ("kh", kh_s), ("v", v_s))}
             for c in range(nck)]
    els = [el_s[c:c + 1, :] for c in range(nck)]

    def part(name, c, h):
        return slabs[c][name][:, h * A_HEAD:(h + 1) * A_HEAD]

    gms = [jnp.where(tri_mask,
                     _dot_nt(jnp.concatenate([part("at", c, h), part("rt", c, h)], axis=0),
                             jnp.concatenate([part("bt", c, h), part("kt", c, h)], axis=0)), 0.0)
           for c, h in chains]
    mvs = [_dot(gm[:, CHUNK:2 * CHUNK], part("v", c, h)) for gm, (c, h) in zip(gms, chains)]
    khv = [_dot_tn(part("kh", c, h), part("v", c, h)) for c, h in chains]
    pws = [gm[0:CHUNK, 0:CHUNK] for gm in gms]
    tms = [eye + pw for pw in pws]
    for _ in range(5):
        pws = [_dot(pw, pw) for pw in pws]
        tms = [tm + _dot(tm, pw) for tm, pw in zip(tms, pws)]
    p12s = [_dot(tm, jnp.concatenate([mv[0:CHUNK], part("at", c, h)], axis=1))
            for tm, mv, (c, h) in zip(tms, mvs, chains)]
    yqs = [jnp.concatenate([mv[CHUNK:2 * CHUNK], part("rt", c, h)], axis=1)
           + _dot(gm[CHUNK:2 * CHUNK, 0:CHUNK], p12)
           for mv, gm, p12, (c, h) in zip(mvs, gms, p12s, chains)]
    zws = [jnp.concatenate(
               [kv, jnp.where(eye_mask,
                              jnp.broadcast_to(els[c][:, h * A_HEAD:(h + 1) * A_HEAD], (CHUNK, CHUNK)), 0.0)],
               axis=1)
           + _dot_tn(part("bh", c, h), p12)
           for kv, p12, (c, h) in zip(khv, p12s, chains)]

    zs = [z_ref[h] for h in heads]
    for c in range(nck):
        ys = []
        for h in heads:
            yq, zw = yqs[c * A_HEADS + h], zws[c * A_HEADS + h]
            ys.append(yq[:, 0:CHUNK] + _dot(yq[:, CHUNK:2 * CHUNK], zs[h]))
            zs[h] = zw[:, 0:CHUNK] + _dot(zw[:, CHUNK:2 * CHUNK], zs[h])
        y_s[rows[c], :] = jnp.concatenate(ys, axis=1)
    for h in heads:
        z_ref[h] = zs[h]

    y = y_s[...]
    inv_n = 1.0 / A_HEAD
    mean = _dot_exact_lhs(y, bd) * inv_n
    yc = y - mean
    var = _dot_exact_lhs(yc * yc, bd) * inv_n
    yn = yc * lax.rsqrt(var + A_GN_EPS) * gng_ref[...] + gnb_ref[...]
    ya_ref[0] = (yn + bonus) * gate


def rwkv7(pa3, mu, w0, w2, a0, a2, g2, kkw, kaw, rk, gng, gnb, tb=256):
    bsz, t, _ = pa3.shape
    full = lambda bi, ti: (0, 0)
    vec = pl.BlockSpec((1, A_WIDTH), full)
    slab = pltpu.VMEM((tb, A_WIDTH), F32)
    return pl.pallas_call(
        functools.partial(_rwkv_kernel, tb=tb),
        grid=(bsz, t // tb),
        in_specs=[
            pl.BlockSpec((1, tb, A_PAD), lambda bi, ti: (bi, ti, 0)),
            pl.BlockSpec((1, A_PAD), full),
            vec,
            pl.BlockSpec((LANES, A_WIDTH), full),
            vec,
            pl.BlockSpec((LANES, A_WIDTH), full),
            pl.BlockSpec((LANES, A_WIDTH), full),
            vec, vec, vec, vec, vec,
        ],
        out_specs=pl.BlockSpec((1, tb, A_WIDTH), lambda bi, ti: (bi, ti, 0)),
        out_shape=jax.ShapeDtypeStruct((bsz, t, A_WIDTH), F32),
        scratch_shapes=[
            pltpu.VMEM((8, A_PAD), F32),
            pltpu.VMEM((A_HEADS, A_HEAD, A_HEAD), F32),
            slab, slab, slab, slab, slab, slab, slab,
            pltpu.VMEM((8, A_WIDTH), F32),
            slab,
        ],
        compiler_params=pltpu.CompilerParams(
            dimension_semantics=("arbitrary", "arbitrary"), vmem_limit_bytes=VMEM_LIMIT),
        name="rwkv7",
    )(pa3, mu, w0, w2, a0, a2, g2, kkw, kaw, rk, gng, gnb)


def _gla_kernel(pb_ref, cw_ref, gw_ref, gb_ref, ng_ref, yb_ref,
                xpad_ref, s_ref, qt_s, kt_s, kh_s, v_s, gk_s, o_s, *, tb):
    nck = tb // CHUNK

    @pl.when(pl.program_id(1) == 0)
    def _():
        xpad_ref[0:8, :] = jnp.zeros((8, B_QKV), F32)
        s_ref[...] = jnp.zeros_like(s_ref)

    p = pb_ref[0]
    xpad_ref[8:8 + tb, :] = p[:, 0:B_QKV]
    conv = cw_ref[B_CONV - 1:B_CONV, :] * p[:, 0:B_QKV]
    for j in range(B_CONV - 1):
        sh = B_CONV - 1 - j
        conv = conv + cw_ref[j:j + 1, :] * xpad_ref[8 - sh:8 - sh + tb, :]
    xpad_ref[0:8, :] = p[tb - 8:tb, 0:B_QKV]
    qkv = _silu(conv)
    q = qkv[:, 0:B_WIDTH_K] * (B_DK ** -0.5)
    k = qkv[:, B_WIDTH_K:2 * B_WIDTH_K]
    v_s[...] = qkv[:, 2 * B_WIDTH_K:]
    gkd = p[:, B_QKV:B_QKV + LANES]
    gz = _dot(gkd, gw_ref[...]) + gb_ref[...]
    gk = -_softplus(-gz) * (1.0 / B_GATE_NORM)
    gk_s[...] = gk
    gcum = _dot_exact_rhs(_chunk_tril(tb, CHUNK), gk)
    qt_s[...] = q * jnp.exp(gcum)
    kt_s[...] = k * jnp.exp(-gcum)
    for c in range(nck):
        rows = slice(c * CHUNK, (c + 1) * CHUNK)
        g_end = gcum[(c + 1) * CHUNK - 1:(c + 1) * CHUNK, :]
        kh_s[rows, :] = k[rows, :] * jnp.exp(g_end - gcum[rows, :])

    ri = lax.broadcasted_iota(I32, (CHUNK, CHUNK), 0)
    ci = lax.broadcasted_iota(I32, (CHUNK, CHUNK), 1)
    causal = ci <= ri
    ones_cv = jnp.ones((CHUNK, B_DV), F32)

    heads = range(B_HEADS)
    chains = [(c, h) for c in range(nck) for h in heads]
    rows = [slice(c * CHUNK, (c + 1) * CHUNK) for c in range(nck)]
    slabs = [{name: ref[rows[c], :] for name, ref in
              (("qt", qt_s), ("kt", kt_s), ("kh", kh_s), ("v", v_s), ("gk", gk_s))} for c in range(nck)]

    def key_part(name, c, h):
        return slabs[c][name][:, h * B_DK:(h + 1) * B_DK]

    def val_part(c, h):
        return slabs[c]["v"][:, h * B_DV:(h + 1) * B_DV]

    ob = ones_cv.astype(BF16)
    dn = (((0,), (0,)), ((), ()))
    intra = [_dot(jnp.where(causal, _dot_nt(key_part("qt", c, h), key_part("kt", c, h)), 0.0), val_part(c, h))
             for c, h in chains]
    kvs = [_dot_tn(key_part("kh", c, h), val_part(c, h)) for c, h in chains]
    decays = []
    for c, h in chains:
        hi, mid, lo = _split3(key_part("gk", c, h))
        gcol = (lax.dot_general(hi, ob, dn, preferred_element_type=F32)
                + lax.dot_general(mid, ob, dn, preferred_element_type=F32)
                + lax.dot_general(lo, ob, dn, preferred_element_type=F32))
        decays.append(jnp.exp(gcol))
    ss = [s_ref[h] for h in heads]
    for c in range(nck):
        os_ = []
        for h in heads:
            n = c * B_HEADS + h
            os_.append(intra[n] + _dot(key_part("qt", c, h), ss[h]))
            ss[h] = ss[h] * decays[n] + kvs[n]
        o_s[rows[c], :] = jnp.concatenate(os_, axis=1)
    for h in heads:
        s_ref[h] = ss[h]

    g = p[:, B_QKV + LANES:]
    outs = []
    for h in range(B_HEADS):
        sv = slice(h * B_DV, (h + 1) * B_DV)
        o = o_s[:, sv]
        o = o * lax.rsqrt(jnp.mean(o * o, axis=-1, keepdims=True) + B_RMS_EPS) * ng_ref[...]
        outs.append(o * _silu(g[:, sv]))
    yb_ref[0] = jnp.concatenate(outs, axis=1)


def gla(pb3, cw, gw, gb, ng, tb=256):
    bsz, t, _ = pb3.shape
    full = lambda bi, ti: (0, 0)
    return pl.pallas_call(
        functools.partial(_gla_kernel, tb=tb),
        grid=(bsz, t // tb),
        in_specs=[
            pl.BlockSpec((1, tb, B_PAD), lambda bi, ti: (bi, ti, 0)),
            pl.BlockSpec((B_CONV, B_QKV), full),
            pl.BlockSpec((LANES, B_WIDTH_K), full),
            pl.BlockSpec((1, B_WIDTH_K), full),
            pl.BlockSpec((1, B_DV), full),
        ],
        out_specs=pl.BlockSpec((1, tb, B_WIDTH), lambda bi, ti: (bi, ti, 0)),
        out_shape=jax.ShapeDtypeStruct((bsz, t, B_WIDTH), F32),
        scratch_shapes=[
            pltpu.VMEM((tb + 8, B_QKV), F32),
            pltpu.VMEM((B_HEADS, B_DK, B_DV), F32),
            pltpu.VMEM((tb, B_WIDTH_K), F32),
            pltpu.VMEM((tb, B_WIDTH_K), F32),
            pltpu.VMEM((tb, B_WIDTH_K), F32),
            pltpu.VMEM((tb, B_WIDTH), F32),
            pltpu.VMEM((tb, B_WIDTH_K), F32),
            pltpu.VMEM((tb, B_WIDTH), F32),
        ],
        compiler_params=pltpu.CompilerParams(
            dimension_semantics=("arbitrary", "arbitrary"), vmem_limit_bytes=VMEM_LIMIT),
        name="gla",
    )(pb3, cw, gw, gb, ng)


def _out_ln_kernel(ya_ref, yb_ref, h_ref, wa_ref, wb_ref, g_ref, b_ref, h1_ref, h1b_ref):
    y = _dot(ya_ref[...], wa_ref[...]) + _dot(yb_ref[...], wb_ref[...])
    h1 = _layer_norm(ALPHA * h_ref[...] + y, g_ref[...], b_ref[...])
    h1_ref[...] = h1
    h1b_ref[...] = h1.astype(BF16)


def out_ln(ya, yb, h, wa, wb, g, b, tm=512):
    n = h.shape[0]
    full = lambda i: (0, 0)
    return pl.pallas_call(
        _out_ln_kernel,
        grid=(n // tm,),
        in_specs=[
            pl.BlockSpec((tm, A_WIDTH), lambda i: (i, 0)),
            pl.BlockSpec((tm, B_WIDTH), lambda i: (i, 0)),
            pl.BlockSpec((tm, D_MODEL), lambda i: (i, 0)),
            pl.BlockSpec((A_WIDTH, D_MODEL), full),
            pl.BlockSpec((B_WIDTH, D_MODEL), full),
            pl.BlockSpec((1, D_MODEL), full),
            pl.BlockSpec((1, D_MODEL), full),
        ],
        out_specs=[
            pl.BlockSpec((tm, D_MODEL), lambda i: (i, 0)),
            pl.BlockSpec((tm, D_MODEL), lambda i: (i, 0)),
        ],
        out_shape=[
            jax.ShapeDtypeStruct((n, D_MODEL), F32),
            jax.ShapeDtypeStruct((n, D_MODEL), BF16),
        ],
        compiler_params=pltpu.CompilerParams(
            dimension_semantics=("arbitrary",), vmem_limit_bytes=VMEM_LIMIT),
        name="out_ln",
    )(ya, yb, h, wa, wb, g, b)


def _fold_keys_kernel(keys_ref, wq_ref, out_ref):
    kh, km, kl = _split3(keys_ref[0])
    wh, wm, wl = _split3(wq_ref[...])
    dn = (((1,), (1,)), ((), ()))
    acc = lax.dot_general(kh, wh, dn, preferred_element_type=F32)
    for a, b in ((kh, wm), (km, wh), (km, wm), (kh, wl), (kl, wh)):
        acc += lax.dot_general(a, b, dn, preferred_element_type=F32)
    out_ref[...] = acc.astype(BF16)


def fold_keys(keys3, wq):
    nhp = keys3.shape[0]
    return pl.pallas_call(
        _fold_keys_kernel,
        grid=(nhp,),
        in_specs=[
            pl.BlockSpec((1, P_KEYS, P_HALF), lambda i: (i, 0, 0)),
            pl.BlockSpec((D_MODEL, P_HALF), lambda i: (0, i)),
        ],
        out_specs=pl.BlockSpec((P_KEYS, D_MODEL), lambda i: (i, 0)),
        out_shape=jax.ShapeDtypeStruct((nhp * P_KEYS, D_MODEL), BF16),
        compiler_params=pltpu.CompilerParams(dimension_semantics=("arbitrary",)),
        name="fold_keys",
    )(keys3, wq)


def _cand_index(tt):
    r = lax.broadcasted_iota(I32, (72, tt), 0)
    c = jnp.where(r < 24, r, jnp.where(r < 32, r + 8, r + 16))
    c = jnp.where(r >= 40, (r - 40) * P_TOPK, c)
    c = jnp.where(r >= 56, (r - 56) * P_TOPK + 1, c)
    c = jnp.where(r >= 64, (r - 64) * P_TOPK + 2, c)
    pos = jnp.where(r >= 64, r - 64, jnp.where(r >= 56, r - 56, jnp.where(r >= 40, r - 40, 4)))
    return c.astype(F32), pos < 4


def _pick16(table, sel):
    out = jnp.zeros_like(table)
    for i in range(P_TOPK):
        out = jnp.where(sel == float(i), table[i:i + 1, :], out)
    return out


def _top16_steps(vals, idx):
    out_v, out_i = [], []
    for _ in range(P_TOPK):
        m = jnp.max(vals, axis=0, keepdims=True)
        i = jnp.min(jnp.where(vals == m, idx, float(2 ** 20)), axis=0, keepdims=True)
        out_v.append(m)
        out_i.append(i)
        vals = jnp.where(idx == i, -jnp.inf, vals)
        yield
    return jnp.concatenate(out_v, axis=0), jnp.concatenate(out_i, axis=0)


def _head_topk_steps(h, half, s_scr, e1s, e2s, gts, key_idx, cand_idx, cand_dup):
    r0 = pl.multiple_of(h * 2 * P_KEYS, 2 * P_KEYS)
    sv1, si1 = yield from _top16_steps(s_scr[half, pl.ds(r0, P_KEYS), :], key_idx)
    sv2, si2 = yield from _top16_steps(s_scr[half, pl.ds(r0 + P_KEYS, P_KEYS), :], key_idx)
    cand = jnp.concatenate(
        [sv1[0:1, :] + sv2]
        + [sv1[i:i + 1, :] + sv2[0:8, :] for i in (1, 2, 3)]
        + [sv1 + sv2[0:1, :]]
        + [sv1[0:8, :] + sv2[j:j + 1, :] for j in (1, 2)], axis=0)
    cv, cidx = yield from _top16_steps(jnp.where(cand_dup, -jnp.inf, cand), cand_idx)
    qi = jnp.floor(cidx * (1.0 / P_TOPK))
    ri = cidx - qi * P_TOPK
    ex = jnp.exp(cv - cv[0:1, :])
    o0 = pl.multiple_of(h * P_TOPK, P_TOPK)
    e1s[half, pl.ds(o0, P_TOPK), :] = _pick16(si1, qi)
    e2s[half, pl.ds(o0, P_TOPK), :] = _pick16(si2, ri)
    gts[half, pl.ds(o0, P_TOPK), :] = ex / jnp.sum(ex, axis=0, keepdims=True)


def _head_topk(*args, **kwargs):
    for _ in _head_topk_steps(*args, **kwargs):
        pass


def _gelu_x2(x):
    return x * (1.0 + lax.erf(x * (1.0 / math.sqrt(2.0))))


def _peer_fused_kernel(h1_ref, h1b_ref, h1bn_ref, ws_ref, u_ref, v_ref, g_ref, b_ref, out_ref,
                       gs_ref, acc_ref, s_scr, e1s, e2s, gts, tm1, tm2, tmg, *, tt, ngrp):
    i = pl.program_id(0)
    j = pl.program_id(1)
    nsel = P_HEADS * P_TOPK
    half_keys = P_KEYS // 2
    th = tt // 2
    key_idx = lax.broadcasted_iota(I32, (P_KEYS, th), 0).astype(F32)
    cand_idx, cand_dup = _cand_index(th)
    topk = functools.partial(_head_topk, s_scr=s_scr, e1s=e1s, e2s=e2s, gts=gts,
                             key_idx=key_idx, cand_idx=cand_idx, cand_dup=cand_dup)

    def scores(hb_ref):
        for half in range(2):
            s_scr[half] = lax.dot_general(ws_ref[...], hb_ref[half * th:(half + 1) * th, :],
                                          (((1,), (1,)), ((), ())), preferred_element_type=F32)

    @pl.when((i == 0) & (j == 0))
    def _():
        scores(h1b_ref)

        def body(t, carry):
            topk(t // 2, t % 2)
            return carry

        lax.fori_loop(0, 2 * P_HEADS, body, 0)

    @pl.when(j == 0)
    def _():
        acc_ref[...] = jnp.zeros_like(acc_ref)
        for half in range(2):
            rows = slice(half * th, (half + 1) * th)
            tm1[rows, :] = e1s[half].T
            tm2[rows, :] = e2s[half].T
            tmg[rows, :] = gts[half].T
        sub = lax.broadcasted_iota(I32, (P_KEYS, nsel), 0).astype(F32)
        zero = jnp.zeros((P_KEYS, nsel), F32)

        def pair_body(m, carry):
            n0 = 2 * m
            onehots = []
            gated = []
            for q in range(2):
                e1 = tm1[pl.ds(n0 + q, 1), :]
                e2 = tm2[pl.ds(n0 + q, 1), :]
                gt = 0.5 * tmg[pl.ds(n0 + q, 1), :]
                gated.append(jnp.where(sub == e1, gt, 0.0))
                onehots.append(jnp.where(sub == e2, 1.0, 0.0))
            a_cat = jnp.concatenate(gated, axis=1).astype(BF16)
            b_bd = jnp.concatenate(
                [jnp.concatenate([onehots[0], zero], axis=1),
                 jnp.concatenate([zero, onehots[1]], axis=1)], axis=0).astype(BF16)
            gn = lax.dot_general(a_cat, b_bd, (((1,), (1,)), ((), ())),
                                 preferred_element_type=F32)
            bits = pltpu.bitcast(gn, I32) + 0x8000
            packed = (bits[half_keys:, :] & -65536) | lax.shift_right_logical(bits[:half_keys, :], 16)
            for q in range(2):
                r0 = pl.multiple_of((n0 + q) * G_PITCH, 8)
                gs_ref[pl.ds(r0, half_keys), :] = packed[:, q * P_KEYS:(q + 1) * P_KEYS]
            return carry

        lax.fori_loop(0, tt // 2, pair_body, 0, unroll=32)
        scores(h1bn_ref)

    nper = 2 * P_HEADS // (half_keys // ngrp)
    units = [_head_topk_steps((j * nper + u) // 2, (j * nper + u) % 2, s_scr, e1s, e2s, gts,
                              key_idx, cand_idx, cand_dup) for u in range(nper)]
    msplit = 2
    npiece = msplit * (2 * (ngrp // 2) + 2)
    per_piece = -(-3 * P_TOPK * nper // npiece)
    tm = tt // msplit

    def advance(count):
        for _ in range(count):
            while units:
                try:
                    next(units[0])
                    break
                except StopIteration:
                    units.pop(0)

    sub = 2 * P_KEYS
    for mi in range(msplit):
        trow = slice(mi * tm, (mi + 1) * tm)
        x = h1b_ref[trow, :]
        h_lo, h_hi = [], []
        for k in range(ngrp // 2):
            words = [gs_ref[pl.ds((mi * tm) * G_PITCH + j * ngrp + 2 * k + q, tm, stride=G_PITCH), :]
                     for q in range(2)]
            cols = slice(k * sub, (k + 1) * sub)
            cols_hi = slice(ngrp * P_KEYS + k * sub, ngrp * P_KEYS + (k + 1) * sub)
            g_lo = jnp.concatenate([pltpu.bitcast(w << 16, F32) for w in words], axis=1)
            s_lo = jnp.dot(x, u_ref[:, cols], preferred_element_type=F32)
            h_lo.append((g_lo * _gelu_x2(s_lo)).astype(BF16))
            advance(per_piece)
            g_hi = jnp.concatenate([pltpu.bitcast(w & -65536, F32) for w in words], axis=1)
            s_hi = jnp.dot(x, u_ref[:, cols_hi], preferred_element_type=F32)
            h_hi.append((g_hi * _gelu_x2(s_hi)).astype(BF16))
            advance(per_piece)
        nlo = ngrp * P_KEYS
        acc = acc_ref[trow, :]
        acc = acc + jnp.dot(jnp.concatenate(h_lo, axis=1), v_ref[0:nlo, :], preferred_element_type=F32)
        advance(per_piece)
        acc = acc + jnp.dot(jnp.concatenate(h_hi, axis=1), v_ref[nlo:2 * nlo, :], preferred_element_type=F32)
        advance(per_piece)
        acc_ref[trow, :] = acc
    advance(4 * P_TOPK * nper)

    @pl.when(j == pl.num_programs(1) - 1)
    def _():
        out_ref[...] = _layer_norm(ALPHA * h1_ref[...] + acc_ref[...], g_ref[...], b_ref[...])


def peer_fused(h1, h1b, ws, ubt, vb, g, b, tt=512, ngrp=PEER_NGRP):
    n = h1.shape[0]
    nsel = P_HEADS * P_TOPK
    nrow = P_HEADS * 2 * P_KEYS
    ntile = n // tt
    nstep = P_KEYS // 2 // ngrp
    assert (2 * P_HEADS) % nstep == 0, "each step runs a whole number of (head, token half) top-k units"
    rows = ngrp * P_KEYS
    th = tt // 2
    tok = lambda i, j: (i, 0)
    full = lambda i, j: (0, 0)
    once = pl.Buffered(1)
    sel = pltpu.VMEM((2, nsel, th), F32)
    tmaj = pltpu.VMEM((tt, nsel), F32)
    return pl.pallas_call(
        functools.partial(_peer_fused_kernel, tt=tt, ngrp=ngrp),
        grid=(ntile, nstep),
        in_specs=[
            pl.BlockSpec((tt, D_MODEL), tok, pipeline_mode=once),
            pl.BlockSpec((tt, D_MODEL), tok, pipeline_mode=once),
            pl.BlockSpec((tt, D_MODEL), lambda i, j: (jnp.minimum(i + 1, ntile - 1), 0), pipeline_mode=once),
            pl.BlockSpec((nrow, D_MODEL), full, pipeline_mode=once),
            pl.BlockSpec((D_MODEL, 2 * rows), lambda i, j: (0, j)),
            pl.BlockSpec((2 * rows, D_MODEL), lambda i, j: (j, 0)),
            pl.BlockSpec((1, D_MODEL), full),
            pl.BlockSpec((1, D_MODEL), full),
        ],
        out_specs=pl.BlockSpec((tt, D_MODEL), tok),
        out_shape=jax.ShapeDtypeStruct((n, D_MODEL), F32),
        scratch_shapes=[
            pltpu.VMEM((tt * G_PITCH, P_KEYS), I32),
            pltpu.VMEM((tt, D_MODEL), F32),
            pltpu.VMEM((2, nrow, th), F32),
            sel, sel, sel,
            tmaj, tmaj, tmaj,
        ],
        compiler_params=pltpu.CompilerParams(
            dimension_semantics=("arbitrary", "arbitrary"), vmem_limit_bytes=VMEM_LIMIT),
        name="peer_fused",
    )(h1, h1b, h1b, ws, ubt, vb, g, b)


def _pad_cols(w, width):
    return jnp.pad(w, ((0, 0), (0, width - w.shape[1])))


def _pad_rows(w, height):
    return jnp.pad(w, ((0, height - w.shape[0]), (0, 0)))


def _layout_a(w):
    w3 = 3 * A_WIDTH
    return jnp.concatenate([
        w[:, 0:w3], _pad_cols(w[:, w3:w3 + 64], LANES), _pad_cols(w[:, w3 + 64:w3 + 128], LANES),
        w[:, w3 + 128:A_IN]], axis=1)


def _layout_b(w):
    return jnp.concatenate([
        w[:, 0:B_QKV], _pad_cols(w[:, B_QKV:B_QKV + B_GATE_LORA], LANES),
        w[:, B_QKV + B_GATE_LORA:]], axis=1)


def _step_order(tab, ngrp):
    nstep = P_KEYS // 2 // ngrp
    rows = ngrp * P_KEYS
    d = tab.shape[1]
    return tab.reshape(2, nstep, rows, d).transpose(1, 0, 2, 3).reshape(2 * nstep * rows, d)


def kernel(x, ln_in_g, ln_in_b, w_in, a_mu, a_w0, a_w2, a_a0, a_a2, a_g2, a_k_k, a_k_a, a_r_k,
           a_gn_g, a_gn_b, b_conv, b_gk_w2, b_gk_b, b_norm_g, w_out, ln1_g, ln1_b, p_wq, p_keys,
           p_u, p_v, ln2_g, ln2_b):
    bsz, t, d = x.shape
    n = bsz * t
    row = lambda a: a.reshape(1, -1)

    assert DEPTH == 1 and w_in.shape[0] == 1, "the entry LayerNorm is fused with the single layer's projection"
    l = 0
    wa = _layout_a(w_in[l][:, :A_IN]).astype(BF16)
    wb = _layout_b(w_in[l][:, A_IN:]).astype(BF16)
    h, pa, pb = ln_proj(x.reshape(n, d), row(ln_in_g), row(ln_in_b), wa, wb)
    ya = rwkv7(
        pa.reshape(bsz, t, A_PAD), _layout_a(row(a_mu[l])), row(a_w0[l]),
        _pad_rows(a_w2[l], LANES), row(a_a0[l]), _pad_rows(a_a2[l], LANES), a_g2[l],
        row(a_k_k[l]), row(a_k_a[l]), row(a_r_k[l]), row(a_gn_g[l]), row(a_gn_b[l]))
    yb = gla(pb.reshape(bsz, t, B_PAD), b_conv[l], _pad_rows(b_gk_w2[l], LANES),
             row(b_gk_b[l]), row(b_norm_g[l]))
    h1, h1b = out_ln(ya.reshape(n, A_WIDTH), yb.reshape(n, B_WIDTH), h,
                     w_out[l][:A_WIDTH].astype(BF16), w_out[l][A_WIDTH:].astype(BF16),
                     row(ln1_g[l]), row(ln1_b[l]))
    ws = fold_keys(p_keys[l].reshape(P_HEADS * 2, P_KEYS, P_HALF), p_wq[l])
    out = peer_fused(h1, h1b, ws, _step_order(p_u[l].astype(BF16), PEER_NGRP).T,
                     _step_order(p_v[l].astype(BF16), PEER_NGRP), row(ln2_g[l]), row(ln2_b[l]),
                     ngrp=PEER_NGRP)
    return out.reshape(bsz, t, d)
```

```python
import functools
import math

import jax
import jax.numpy as jnp
from jax import lax
from jax.experimental import pallas as pl
from jax.experimental.pallas import tpu as pltpu

F32 = jnp.float32
BF16 = jnp.bfloat16
I32 = jnp.int32

D_MODEL = 1024
DEPTH = 1
CHUNK = 64
LN_EPS = 1e-5
ALPHA = (2.0 * DEPTH) ** 0.25

A_HEAD = 64
A_WIDTH = 512
A_HEADS = 8
A_GN_EPS = 64e-5
A_IN = 1792
A_PAD = 1920

B_HEADS = 4
B_DV = 128
B_DK = 64
B_WIDTH = 512
B_WIDTH_K = 256
B_QKV = 1024
B_GATE_LORA = 16
B_GATE_NORM = 16.0
B_CONV = 4
B_RMS_EPS = 1e-5
B_PAD = 1664

P_HEADS = 8
P_KEYS = 128
P_HALF = 128
P_TOPK = 16

LANES = 128
G_PITCH = 72

VMEM_LIMIT = 56 * 1024 * 1024


def _dot(a, b):
    return jnp.dot(a.astype(BF16), b.astype(BF16), preferred_element_type=F32)


def _dot_nt(a, b):
    return lax.dot_general(a.astype(BF16), b.astype(BF16), (((1,), (1,)), ((), ())),
                           preferred_element_type=F32)


def _dot_tn(a, b):
    return lax.dot_general(a.astype(BF16), b.astype(BF16), (((0,), (0,)), ((), ())),
                           preferred_element_type=F32)


def _split3(x):
    hi = x.astype(BF16)
    r1 = x - hi.astype(F32)
    mid = r1.astype(BF16)
    lo = (r1 - mid.astype(F32)).astype(BF16)
    return hi, mid, lo


def _dot_exact_rhs(a01, x):
    a = a01.astype(BF16)
    hi, mid, lo = _split3(x)
    out = jnp.dot(a, hi, preferred_element_type=F32)
    out += jnp.dot(a, mid, preferred_element_type=F32)
    out += jnp.dot(a, lo, preferred_element_type=F32)
    return out


def _dot_exact_lhs(x, b01):
    b = b01.astype(BF16)
    hi, mid, lo = _split3(x)
    out = jnp.dot(hi, b, preferred_element_type=F32)
    out += jnp.dot(mid, b, preferred_element_type=F32)
    out += jnp.dot(lo, b, preferred_element_type=F32)
    return out


def _sigmoid(x):
    return 1.0 / (1.0 + jnp.exp(-x))


def _softplus(x):
    return jnp.maximum(x, 0.0) + jnp.log(1.0 + jnp.exp(-jnp.abs(x)))


def _silu(x):
    return x * _sigmoid(x)


def _layer_norm(x, g, b):
    mu = jnp.mean(x, axis=-1, keepdims=True)
    xc = x - mu
    var = jnp.mean(xc * xc, axis=-1, keepdims=True)
    return xc * lax.rsqrt(var + LN_EPS) * g + b


def _block_diag_ones(n, seg):
    r = lax.broadcasted_iota(I32, (n, n), 0) // seg
    c = lax.broadcasted_iota(I32, (n, n), 1) // seg
    return jnp.where(r == c, 1.0, 0.0).astype(F32)


def _chunk_tril(n, seg):
    r = lax.broadcasted_iota(I32, (n, n), 0)
    c = lax.broadcasted_iota(I32, (n, n), 1)
    return jnp.where((r // seg == c // seg) & (c <= r), 1.0, 0.0).astype(F32)


def _ln_proj_kernel(x_ref, g_ref, b_ref, wa_ref, wb_ref, h_ref, pa_ref, pb_ref):
    h = _layer_norm(x_ref[...], g_ref[...], b_ref[...])
    h_ref[...] = h
    hb = h.astype(BF16)
    pa_ref[...] = jnp.dot(hb, wa_ref[...], preferred_element_type=F32)
    pb_ref[...] = jnp.dot(hb, wb_ref[...], preferred_element_type=F32)


def ln_proj(x2, g, b, wa, wb, tm=512):
    n = x2.shape[0]
    full = lambda i: (0, 0)
    return pl.pallas_call(
        _ln_proj_kernel,
        grid=(n // tm,),
        in_specs=[
            pl.BlockSpec((tm, D_MODEL), lambda i: (i, 0)),
            pl.BlockSpec((1, D_MODEL), full),
            pl.BlockSpec((1, D_MODEL), full),
            pl.BlockSpec((D_MODEL, A_PAD), full),
            pl.BlockSpec((D_MODEL, B_PAD), full),
        ],
        out_specs=[
            pl.BlockSpec((tm, D_MODEL), lambda i: (i, 0)),
            pl.BlockSpec((tm, A_PAD), lambda i: (i, 0)),
            pl.BlockSpec((tm, B_PAD), lambda i: (i, 0)),
        ],
        out_shape=[
            jax.ShapeDtypeStruct((n, D_MODEL), F32),
            jax.ShapeDtypeStruct((n, A_PAD), F32),
            jax.ShapeDtypeStruct((n, B_PAD), F32),
        ],
        compiler_params=pltpu.CompilerParams(
            dimension_semantics=("arbitrary",), vmem_limit_bytes=VMEM_LIMIT),
        name="ln_proj",
    )(x2, g, b, wa, wb)


def _rwkv_kernel(pa_ref, mu_ref, w0_ref, w2_ref, a0_ref, a2_ref, g2_ref, kkw_ref, kaw_ref,
                 rk_ref, gng_ref, gnb_ref, ya_ref,
                 carry_ref, z_ref, at_s, rt_s, bt_s, kt_s, bh_s, kh_s, v_s, el_s, y_s, *, tb):
    nck = tb // CHUNK
    w = A_WIDTH

    @pl.when(pl.program_id(1) == 0)
    def _():
        carry_ref[...] = jnp.zeros_like(carry_ref)
        z_ref[...] = jnp.zeros_like(z_ref)

    p = pa_ref[0]
    rolled = pltpu.roll(p, 1, axis=0)
    row = lax.broadcasted_iota(I32, p.shape, 0)
    prev = jnp.where(row == 0, carry_ref[0:1, :], rolled)
    carry_ref[0:1, :] = p[tb - 1:tb, :]
    ps = p + (prev - p) * mu_ref[...]

    r = ps[:, 0:w]
    k = ps[:, w:2 * w]
    v = ps[:, 2 * w:3 * w]
    wd = ps[:, 3 * w:3 * w + LANES]
    ad = ps[:, 3 * w + LANES:3 * w + 2 * LANES]
    gd = ps[:, 3 * w + 2 * LANES:3 * w + 3 * LANES]

    wlin = w0_ref[...] + _dot(jnp.tanh(wd), w2_ref[...])
    lw = -jnp.exp(-_softplus(-wlin) - 0.5)
    iclr = _sigmoid(a0_ref[...] + _dot(ad, a2_ref[...]))
    gate = _dot(_sigmoid(gd), g2_ref[...])

    bd = _block_diag_ones(w, A_HEAD)
    kkr = k * kkw_ref[...]
    ss = _dot_exact_lhs(kkr * kkr, bd)
    kk = kkr / jnp.maximum(jnp.sqrt(ss), 1e-12)
    kh = k * (1.0 + (iclr - 1.0) * kaw_ref[...])
    b = kk * iclr
    bonus = _dot_exact_lhs(r * kh * rk_ref[...], bd) * v

    lcum = _dot_exact_rhs(_chunk_tril(tb, CHUNK), lw)
    e_l = jnp.exp(lcum)
    e_nl = jnp.exp(-lcum)
    at_s[...] = -kk * jnp.exp(lcum - lw)
    rt_s[...] = r * e_l
    bt_s[...] = b * e_nl
    kt_s[...] = kh * e_nl
    v_s[...] = v
    for c in range(nck):
        rows = slice(c * CHUNK, (c + 1) * CHUNK)
        l_end = lcum[(c + 1) * CHUNK - 1:(c + 1) * CHUNK, :]
        e_lc = jnp.exp(l_end - lcum[rows, :])
        bh_s[rows, :] = b[rows, :] * e_lc
        kh_s[rows, :] = kh[rows, :] * e_lc
        el_s[c:c + 1, :] = jnp.exp(l_end)

    ri = lax.broadcasted_iota(I32, (2 * CHUNK, 2 * CHUNK), 0)
    ci = lax.broadcasted_iota(I32, (2 * CHUNK, 2 * CHUNK), 1) % CHUNK
    tri_mask = ci < jnp.where(ri < CHUNK, ri, ri - (CHUNK - 1))
    er = lax.broadcasted_iota(I32, (CHUNK, CHUNK), 0)
    ec = lax.broadcasted_iota(I32, (CHUNK, CHUNK), 1)
    eye_mask = er == ec
    eye = jnp.where(eye_mask, 1.0, 0.0).astype(F32)

    heads = range(A_HEADS)
    chains = [(c, h) for c in range(nck) for h in heads]
    rows = [slice(c * CHUNK, (c + 1) * CHUNK) for c in range(nck)]
    slabs = [{name: ref[rows[c], :] for name, ref in
              (("at", at_s), ("rt", rt_s), ("bt", bt_s), ("kt", kt_s), ("bh", bh_s), ("kh", kh_s), ("v", v_s))}
             for c in range(nck)]
    els = [el_s[c:c + 1, :] for c in range(nck)]

    def part(name, c, h):
        return slabs[c][name][:, h * A_HEAD:(h + 1) * A_HEAD]

    gms = [jnp.where(tri_mask,
                     _dot_nt(jnp.concatenate([part("at", c, h), part("rt", c, h)], axis=0),
                             jnp.concatenate([part("bt", c, h), part("kt", c, h)], axis=0)), 0.0)
           for c, h in chains]
    mvs = [_dot(gm[:, CHUNK:2 * CHUNK], part("v", c, h)) for gm, (c, h) in zip(gms, chains)]
    khv = [_dot_tn(part("kh", c, h), part("v", c, h)) for c, h in chains]
    pws = [gm[0:CHUNK, 0:CHUNK] for gm in gms]
    tms = [eye + pw for pw in pws]
    for _ in range(5):
        pws = [_dot(pw, pw) for pw in pws]
        tms = [tm + _dot(tm, pw) for tm, pw in zip(tms, pws)]
    p12s = [_dot(tm, jnp.concatenate([mv[0:CHUNK], part("at", c, h)], axis=1))
            for tm, mv, (c, h) in zip(tms, mvs, chains)]
    yqs = [jnp.concatenate([mv[CHUNK:2 * CHUNK], part("rt", c, h)], axis=1)
           + _dot(gm[CHUNK:2 * CHUNK, 0:CHUNK], p12)
           for mv, gm, p12, (c, h) in zip(mvs, gms, p12s, chains)]
    zws = [jnp.concatenate(
               [kv, jnp.where(eye_mask,
                              jnp.broadcast_to(els[c][:, h * A_HEAD:(h + 1) * A_HEAD], (CHUNK, CHUNK)), 0.0)],
               axis=1)
           + _dot_tn(part("bh", c, h), p12)
           for kv, p12, (c, h) in zip(khv, p12s, chains)]

    zs = [z_ref[h] for h in heads]
    for c in range(nck):
        ys = []
        for h in heads:
            yq, zw = yqs[c * A_HEADS + h], zws[c * A_HEADS + h]
            ys.append(yq[:, 0:CHUNK] + _dot(yq[:, CHUNK:2 * CHUNK], zs[h]))
            zs[h] = zw[:, 0:CHUNK] + _dot(zw[:, CHUNK:2 * CHUNK], zs[h])
        y_s[rows[c], :] = jnp.concatenate(ys, axis=1)
    for h in heads:
        z_ref[h] = zs[h]

    y = y_s[...]
    inv_n = 1.0 / A_HEAD
    mean = _dot_exact_lhs(y, bd) * inv_n
    yc = y - mean
    var = _dot_exact_lhs(yc * yc, bd) * inv_n
    yn = yc * lax.rsqrt(var + A_GN_EPS) * gng_ref[...] + gnb_ref[...]
    ya_ref[0] = (yn + bonus) * gate


def rwkv7(pa3, mu, w0, w2, a0, a2, g2, kkw, kaw, rk, gng, gnb, tb=256):
    bsz, t, _ = pa3.shape
    full = lambda bi, ti: (0, 0)
    vec = pl.BlockSpec((1, A_WIDTH), full)
    slab = pltpu.VMEM((tb, A_WIDTH), F32)
    return pl.pallas_call(
        functools.partial(_rwkv_kernel, tb=tb),
        grid=(bsz, t // tb),
        in_specs=[
            pl.BlockSpec((1, tb, A_PAD), lambda bi, ti: (bi, ti, 0)),
            pl.BlockSpec((1, A_PAD), full),
            vec,
            pl.BlockSpec((LANES, A_WIDTH), full),
            vec,
            pl.BlockSpec((LANES, A_WIDTH), full),
            pl.BlockSpec((LANES, A_WIDTH), full),
            vec, vec, vec, vec, vec,
        ],
        out_specs=pl.BlockSpec((1, tb, A_WIDTH), lambda bi, ti: (bi, ti, 0)),
        out_shape=jax.ShapeDtypeStruct((bsz, t, A_WIDTH), F32),
        scratch_shapes=[
            pltpu.VMEM((8, A_PAD), F32),
            pltpu.VMEM((A_HEADS, A_HEAD, A_HEAD), F32),
            slab, slab, slab, slab, slab, slab, slab,
            pltpu.VMEM((8, A_WIDTH), F32),
            slab,
        ],
        compiler_params=pltpu.CompilerParams(
            dimension_semantics=("arbitrary", "arbitrary"), vmem_limit_bytes=VMEM_LIMIT),
        name="rwkv7",
    )(pa3, mu, w0, w2, a0, a2, g2, kkw, kaw, rk, gng, gnb)


def _gla_kernel(pb_ref, cw_ref, gw_ref, gb_ref, ng_ref, yb_ref,
                xpad_ref, s_ref, qt_s, kt_s, kh_s, v_s, gk_s, o_s, *, tb):
    nck = tb // CHUNK

    @pl.when(pl.program_id(1) == 0)
    def _():
        xpad_ref[0:8, :] = jnp.zeros((8, B_QKV), F32)
        s_ref[...] = jnp.zeros_like(s_ref)

    p = pb_ref[0]
    xpad_ref[8:8 + tb, :] = p[:, 0:B_QKV]
    conv = cw_ref[B_CONV - 1:B_CONV, :] * p[:, 0:B_QKV]
    for j in range(B_CONV - 1):
        sh = B_CONV - 1 - j
        conv = conv + cw_ref[j:j + 1, :] * xpad_ref[8 - sh:8 - sh + tb, :]
    xpad_ref[0:8, :] = p[tb - 8:tb, 0:B_QKV]
    qkv = _silu(conv)
    q = qkv[:, 0:B_WIDTH_K] * (B_DK ** -0.5)
    k = qkv[:, B_WIDTH_K:2 * B_WIDTH_K]
    v_s[...] = qkv[:, 2 * B_WIDTH_K:]
    gkd = p[:, B_QKV:B_QKV + LANES]
    gz = _dot(gkd, gw_ref[...]) + gb_ref[...]
    gk = -_softplus(-gz) * (1.0 / B_GATE_NORM)
    gk_s[...] = gk
    gcum = _dot_exact_rhs(_chunk_tril(tb, CHUNK), gk)
    qt_s[...] = q * jnp.exp(gcum)
    kt_s[...] = k * jnp.exp(-gcum)
    for c in range(nck):
        rows = slice(c * CHUNK, (c + 1) * CHUNK)
        g_end = gcum[(c + 1) * CHUNK - 1:(c + 1) * CHUNK, :]
        kh_s[rows, :] = k[rows, :] * jnp.exp(g_end - gcum[rows, :])

    ri = lax.broadcasted_iota(I32, (CHUNK, CHUNK), 0)
    ci = lax.broadcasted_iota(I32, (CHUNK, CHUNK), 1)
    causal = ci <= ri
    ones_cv = jnp.ones((CHUNK, B_DV), F32)

    heads = range(B_HEADS)
    chains = [(c, h) for c in range(nck) for h in heads]
    rows = [slice(c * CHUNK, (c + 1) * CHUNK) for c in range(nck)]
    slabs = [{name: ref[rows[c], :] for name, ref in
              (("qt", qt_s), ("kt", kt_s), ("kh", kh_s), ("v", v_s), ("gk", gk_s))} for c in range(nck)]

    def key_part(name, c, h):
        return slabs[c][name][:, h * B_DK:(h + 1) * B_DK]

    def val_part(c, h):
        return slabs[c]["v"][:, h * B_DV:(h + 1) * B_DV]

    ob = ones_cv.astype(BF16)
    dn = (((0,), (0,)), ((), ()))
    intra = [_dot(jnp.where(causal, _dot_nt(key_part("qt", c, h), key_part("kt", c, h)), 0.0), val_part(c, h))
             for c, h in chains]
    kvs = [_dot_tn(key_part("kh", c, h), val_part(c, h)) for c, h in chains]
    decays = []
    for c, h in chains:
        hi, mid, lo = _split3(key_part("gk", c, h))
        gcol = (lax.dot_general(hi, ob, dn, preferred_element_type=F32)
                + lax.dot_general(mid, ob, dn, preferred_element_type=F32)
                + lax.dot_general(lo, ob, dn, preferred_element_type=F32))
        decays.append(jnp.exp(gcol))
    ss = [s_ref[h] for h in heads]
    for c in range(nck):
        os_ = []
        for h in heads:
            n = c * B_HEADS + h
            os_.append(intra[n] + _dot(key_part("qt", c, h), ss[h]))
            ss[h] = ss[h] * decays[n] + kvs[n]
        o_s[rows[c], :] = jnp.concatenate(os_, axis=1)
    for h in heads:
        s_ref[h] = ss[h]

    g = p[:, B_QKV + LANES:]
    outs = []
    for h in range(B_HEADS):
        sv = slice(h * B_DV, (h + 1) * B_DV)
        o = o_s[:, sv]
        o = o * lax.rsqrt(jnp.mean(o * o, axis=-1, keepdims=True) + B_RMS_EPS) * ng_ref[...]
        outs.append(o * _silu(g[:, sv]))
    yb_ref[0] = jnp.concatenate(outs, axis=1)


def gla(pb3, cw, gw, gb, ng, tb=256):
    bsz, t, _ = pb3.shape
    full = lambda bi, ti: (0, 0)
    return pl.pallas_call(
        functools.partial(_gla_kernel, tb=tb),
        grid=(bsz, t // tb),
        in_specs=[
            pl.BlockSpec((1, tb, B_PAD), lambda bi, ti: (bi, ti, 0)),
            pl.BlockSpec((B_CONV, B_QKV), full),
            pl.BlockSpec((LANES, B_WIDTH_K), full),
            pl.BlockSpec((1, B_WIDTH_K), full),
            pl.BlockSpec((1, B_DV), full),
        ],
        out_specs=pl.BlockSpec((1, tb, B_WIDTH), lambda bi, ti: (bi, ti, 0)),
        out_shape=jax.ShapeDtypeStruct((bsz, t, B_WIDTH), F32),
        scratch_shapes=[
            pltpu.VMEM((tb + 8, B_QKV), F32),
            pltpu.VMEM((B_HEADS, B_DK, B_DV), F32),
            pltpu.VMEM((tb, B_WIDTH_K), F32),
            pltpu.VMEM((tb, B_WIDTH_K), F32),
            pltpu.VMEM((tb, B_WIDTH_K), F32),
            pltpu.VMEM((tb, B_WIDTH), F32),
            pltpu.VMEM((tb, B_WIDTH_K), F32),
            pltpu.VMEM((tb, B_WIDTH), F32),
        ],
        compiler_params=pltpu.CompilerParams(
            dimension_semantics=("arbitrary", "arbitrary"), vmem_limit_bytes=VMEM_LIMIT),
        name="gla",
    )(pb3, cw, gw, gb, ng)


def _out_ln_kernel(ya_ref, yb_ref, h_ref, wa_ref, wb_ref, g_ref, b_ref, h1_ref, h1b_ref):
    y = _dot(ya_ref[...], wa_ref[...]) + _dot(yb_ref[...], wb_ref[...])
    h1 = _layer_norm(ALPHA * h_ref[...] + y, g_ref[...], b_ref[...])
    h1_ref[...] = h1
    h1b_ref[...] = h1.astype(BF16)


def out_ln(ya, yb, h, wa, wb, g, b, tm=512):
    n = h.shape[0]
    full = lambda i: (0, 0)
    return pl.pallas_call(
        _out_ln_kernel,
        grid=(n // tm,),
        in_specs=[
            pl.BlockSpec((tm, A_WIDTH), lambda i: (i, 0)),
            pl.BlockSpec((tm, B_WIDTH), lambda i: (i, 0)),
            pl.BlockSpec((tm, D_MODEL), lambda i: (i, 0)),
            pl.BlockSpec((A_WIDTH, D_MODEL), full),
            pl.BlockSpec((B_WIDTH, D_MODEL), full),
            pl.BlockSpec((1, D_MODEL), full),
            pl.BlockSpec((1, D_MODEL), full),
        ],
        out_specs=[
            pl.BlockSpec((tm, D_MODEL), lambda i: (i, 0)),
            pl.BlockSpec((tm, D_MODEL), lambda i: (i, 0)),
        ],
        out_shape=[
            jax.ShapeDtypeStruct((n, D_MODEL), F32),
            jax.ShapeDtypeStruct((n, D_MODEL), BF16),
        ],
        compiler_params=pltpu.CompilerParams(
            dimension_semantics=("arbitrary",), vmem_limit_bytes=VMEM_LIMIT),
        name="out_ln",
    )(ya, yb, h, wa, wb, g, b)


def _fold_keys_kernel(keys_ref, wq_ref, out_ref):
    kh, km, kl = _split3(keys_ref[0])
    wh, wm, wl = _split3(wq_ref[...])
    dn = (((1,), (1,)), ((), ()))
    acc = lax.dot_general(kh, wh, dn, preferred_element_type=F32)
    for a, b in ((kh, wm), (km, wh), (km, wm), (kh, wl), (kl, wh)):
        acc += lax.dot_general(a, b, dn, preferred_element_type=F32)
    out_ref[...] = acc.astype(BF16)


def fold_keys(keys3, wq):
    nhp = keys3.shape[0]
    return pl.pallas_call(
        _fold_keys_kernel,
        grid=(nhp,),
        in_specs=[
            pl.BlockSpec((1, P_KEYS, P_HALF), lambda i: (i, 0, 0)),
            pl.BlockSpec((D_MODEL, P_HALF), lambda i: (0, i)),
        ],
        out_specs=pl.BlockSpec((P_KEYS, D_MODEL), lambda i: (i, 0)),
        out_shape=jax.ShapeDtypeStruct((nhp * P_KEYS, D_MODEL), BF16),
        compiler_params=pltpu.CompilerParams(dimension_semantics=("arbitrary",)),
        name="fold_keys",
    )(keys3, wq)


def _cand_index(tt):
    r = lax.broadcasted_iota(I32, (72, tt), 0)
    c = jnp.where(r < 24, r, jnp.where(r < 32, r + 8, r + 16))
    c = jnp.where(r >= 40, (r - 40) * P_TOPK, c)
    c = jnp.where(r >= 56, (r - 56) * P_TOPK + 1, c)
    c = jnp.where(r >= 64, (r - 64) * P_TOPK + 2, c)
    pos = jnp.where(r >= 64, r - 64, jnp.where(r >= 56, r - 56, jnp.where(r >= 40, r - 40, 4)))
    return c.astype(F32), pos < 4


def _pick16(table, sel):
    out = jnp.zeros_like(table)
    for i in range(P_TOPK):
        out = jnp.where(sel == float(i), table[i:i + 1, :], out)
    return out


def _top16_steps(vals, idx):
    out_v, out_i = [], []
    for _ in range(P_TOPK):
        m = jnp.max(vals, axis=0, keepdims=True)
        i = jnp.min(jnp.where(vals == m, idx, float(2 ** 20)), axis=0, keepdims=True)
        out_v.append(m)
        out_i.append(i)
        vals = jnp.where(idx == i, -jnp.inf, vals)
        yield
    return jnp.concatenate(out_v, axis=0), jnp.concatenate(out_i, axis=0)


def _head_topk_steps(h, half, s_scr, e1s, e2s, gts, key_idx, cand_idx, cand_dup):
    r0 = pl.multiple_of(h * 2 * P_KEYS, 2 * P_KEYS)
    sv1, si1 = yield from _top16_steps(s_scr[half, pl.ds(r0, P_KEYS), :], key_idx)
    sv2, si2 = yield from _top16_steps(s_scr[half, pl.ds(r0 + P_KEYS, P_KEYS), :], key_idx)
    cand = jnp.concatenate(
        [sv1[0:1, :] + sv2]
        + [sv1[i:i + 1, :] + sv2[0:8, :] for i in (1, 2, 3)]
        + [sv1 + sv2[0:1, :]]
        + [sv1[0:8, :] + sv2[j:j + 1, :] for j in (1, 2)], axis=0)
    cv, cidx = yield from _top16_steps(jnp.where(cand_dup, -jnp.inf, cand), cand_idx)
    qi = jnp.floor(cidx * (1.0 / P_TOPK))
    ri = cidx - qi * P_TOPK
    ex = jnp.exp(cv - cv[0:1, :])
    o0 = pl.multiple_of(h * P_TOPK, P_TOPK)
    e1s[half, pl.ds(o0, P_TOPK), :] = _pick16(si1, qi)
    e2s[half, pl.ds(o0, P_TOPK), :] = _pick16(si2, ri)
    gts[half, pl.ds(o0, P_TOPK), :] = ex / jnp.sum(ex, axis=0, keepdims=True)


def _head_topk(*args, **kwargs):
    for _ in _head_topk_steps(*args, **kwargs):
        pass


def _gelu_x2(x):
    return x * (1.0 + lax.erf(x * (1.0 / math.sqrt(2.0))))


def _peer_fused_kernel(h1_ref, h1b_ref, h1bn_ref, ws_ref, ulo_ref, uhi_ref, vlo_ref, vhi_ref,
                       g_ref, b_ref, out_ref,
                       gs_ref, acc_ref, s_scr, e1s, e2s, gts, tm1, tm2, tmg, *, tt, ngrp):
    i = pl.program_id(0)
    j = pl.program_id(1)
    nsel = P_HEADS * P_TOPK
    half_keys = P_KEYS // 2
    th = tt // 2
    key_idx = lax.broadcasted_iota(I32, (P_KEYS, th), 0).astype(F32)
    cand_idx, cand_dup = _cand_index(th)
    topk = functools.partial(_head_topk, s_scr=s_scr, e1s=e1s, e2s=e2s, gts=gts,
                             key_idx=key_idx, cand_idx=cand_idx, cand_dup=cand_dup)

    def scores(hb_ref):
        for half in range(2):
            s_scr[half] = lax.dot_general(ws_ref[...], hb_ref[half * th:(half + 1) * th, :],
                                          (((1,), (1,)), ((), ())), preferred_element_type=F32)

    @pl.when((i == 0) & (j == 0))
    def _():
        scores(h1b_ref)

        def body(t, carry):
            topk(t // 2, t % 2)
            return carry

        lax.fori_loop(0, 2 * P_HEADS, body, 0)

    @pl.when(j == 0)
    def _():
        acc_ref[...] = jnp.zeros_like(acc_ref)
        for half in range(2):
            rows = slice(half * th, (half + 1) * th)
            tm1[rows, :] = e1s[half].T
            tm2[rows, :] = e2s[half].T
            tmg[rows, :] = gts[half].T
        sub = lax.broadcasted_iota(I32, (P_KEYS, nsel), 0).astype(F32)
        zero = jnp.zeros((P_KEYS, nsel), F32)

        def pair_body(m, carry):
            n0 = 2 * m
            onehots = []
            gated = []
            for q in range(2):
                e1 = tm1[pl.ds(n0 + q, 1), :]
                e2 = tm2[pl.ds(n0 + q, 1), :]
                gt = 0.5 * tmg[pl.ds(n0 + q, 1), :]
                gated.append(jnp.where(sub == e1, gt, 0.0))
                onehots.append(jnp.where(sub == e2, 1.0, 0.0))
            a_cat = jnp.concatenate(gated, axis=1).astype(BF16)
            b_bd = jnp.concatenate(
                [jnp.concatenate([onehots[0], zero], axis=1),
                 jnp.concatenate([zero, onehots[1]], axis=1)], axis=0).astype(BF16)
            gn = lax.dot_general(a_cat, b_bd, (((1,), (1,)), ((), ())),
                                 preferred_element_type=F32)
            bits = pltpu.bitcast(gn, I32) + 0x8000
            packed = (bits[half_keys:, :] & -65536) | lax.shift_right_logical(bits[:half_keys, :], 16)
            for q in range(2):
                r0 = pl.multiple_of((n0 + q) * G_PITCH, 8)
                gs_ref[pl.ds(r0, half_keys), :] = packed[:, q * P_KEYS:(q + 1) * P_KEYS]
            return carry

        lax.fori_loop(0, tt // 2, pair_body, 0, unroll=32)
        scores(h1bn_ref)

    nper = 2 * P_HEADS // (half_keys // ngrp)
    units = [_head_topk_steps((j * nper + u) // 2, (j * nper + u) % 2, s_scr, e1s, e2s, gts,
                              key_idx, cand_idx, cand_dup) for u in range(nper)]
    msplit = 2
    npiece = msplit * (2 * (ngrp // 2) + 2)
    per_piece = -(-3 * P_TOPK * nper // npiece)
    tm = tt // msplit

    def advance(count):
        for _ in range(count):
            while units:
                try:
                    next(units[0])
                    break
                except StopIteration:
                    units.pop(0)

    sub = 2 * P_KEYS
    for mi in range(msplit):
        trow = slice(mi * tm, (mi + 1) * tm)
        x = h1b_ref[trow, :]
        h_lo, h_hi = [], []
        for k in range(ngrp // 2):
            words = [gs_ref[pl.ds((mi * tm) * G_PITCH + j * ngrp + 2 * k + q, tm, stride=G_PITCH), :]
                     for q in range(2)]
            cols = slice(k * sub, (k + 1) * sub)
            g_lo = jnp.concatenate([pltpu.bitcast(w << 16, F32) for w in words], axis=1)
            s_lo = jnp.dot(x, ulo_ref[:, cols], preferred_element_type=F32)
            h_lo.append((g_lo * _gelu_x2(s_lo)).astype(BF16))
            advance(per_piece)
            g_hi = jnp.concatenate([pltpu.bitcast(w & -65536, F32) for w in words], axis=1)
            s_hi = jnp.dot(x, uhi_ref[:, cols], preferred_element_type=F32)
            h_hi.append((g_hi * _gelu_x2(s_hi)).astype(BF16))
            advance(per_piece)
        acc = acc_ref[trow, :]
        acc = acc + jnp.dot(jnp.concatenate(h_lo, axis=1), vlo_ref[...], preferred_element_type=F32)
        advance(per_piece)
        acc = acc + jnp.dot(jnp.concatenate(h_hi, axis=1), vhi_ref[...], preferred_element_type=F32)
        advance(per_piece)
        acc_ref[trow, :] = acc
    advance(4 * P_TOPK * nper)

    @pl.when(j == pl.num_programs(1) - 1)
    def _():
        out_ref[...] = _layer_norm(ALPHA * h1_ref[...] + acc_ref[...], g_ref[...], b_ref[...])


def peer_fused(h1, h1b, ws, ubt, vb, g, b, tt=512, ngrp=4):
    n = h1.shape[0]
    nsel = P_HEADS * P_TOPK
    nrow = P_HEADS * 2 * P_KEYS
    ntile = n // tt
    nstep = P_KEYS // 2 // ngrp
    assert (2 * P_HEADS) % nstep == 0, "each step runs a whole number of (head, token half) top-k units"
    rows = ngrp * P_KEYS
    th = tt // 2
    tok = lambda i, j: (i, 0)
    full = lambda i, j: (0, 0)
    once = pl.Buffered(1)
    sel = pltpu.VMEM((2, nsel, th), F32)
    tmaj = pltpu.VMEM((tt, nsel), F32)
    return pl.pallas_call(
        functools.partial(_peer_fused_kernel, tt=tt, ngrp=ngrp),
        grid=(ntile, nstep),
        in_specs=[
            pl.BlockSpec((tt, D_MODEL), tok, pipeline_mode=once),
            pl.BlockSpec((tt, D_MODEL), tok, pipeline_mode=once),
            pl.BlockSpec((tt, D_MODEL), lambda i, j: (jnp.minimum(i + 1, ntile - 1), 0), pipeline_mode=once),
            pl.BlockSpec((nrow, D_MODEL), full, pipeline_mode=once),
            pl.BlockSpec((D_MODEL, rows), lambda i, j: (0, j)),
            pl.BlockSpec((D_MODEL, rows), lambda i, j: (0, j + nstep)),
            pl.BlockSpec((rows, D_MODEL), lambda i, j: (j, 0)),
            pl.BlockSpec((rows, D_MODEL), lambda i, j: (j + nstep, 0)),
            pl.BlockSpec((1, D_MODEL), full),
            pl.BlockSpec((1, D_MODEL), full),
        ],
        out_specs=pl.BlockSpec((tt, D_MODEL), tok),
        out_shape=jax.ShapeDtypeStruct((n, D_MODEL), F32),
        scratch_shapes=[
            pltpu.VMEM((tt * G_PITCH, P_KEYS), I32),
            pltpu.VMEM((tt, D_MODEL), F32),
            pltpu.VMEM((2, nrow, th), F32),
            sel, sel, sel,
            tmaj, tmaj, tmaj,
        ],
        compiler_params=pltpu.CompilerParams(
            dimension_semantics=("arbitrary", "arbitrary"), vmem_limit_bytes=VMEM_LIMIT),
        name="peer_fused",
    )(h1, h1b, h1b, ws, ubt, ubt, vb, vb, g, b)


def _pad_cols(w, width):
    return jnp.pad(w, ((0, 0), (0, width - w.shape[1])))


def _pad_rows(w, height):
    return jnp.pad(w, ((0, height - w.shape[0]), (0, 0)))


def _layout_a(w):
    w3 = 3 * A_WIDTH
    return jnp.concatenate([
        w[:, 0:w3], _pad_cols(w[:, w3:w3 + 64], LANES), _pad_cols(w[:, w3 + 64:w3 + 128], LANES),
        w[:, w3 + 128:A_IN]], axis=1)


def _layout_b(w):
    return jnp.concatenate([
        w[:, 0:B_QKV], _pad_cols(w[:, B_QKV:B_QKV + B_GATE_LORA], LANES),
        w[:, B_QKV + B_GATE_LORA:]], axis=1)


def kernel(x, ln_in_g, ln_in_b, w_in, a_mu, a_w0, a_w2, a_a0, a_a2, a_g2, a_k_k, a_k_a, a_r_k,
           a_gn_g, a_gn_b, b_conv, b_gk_w2, b_gk_b, b_norm_g, w_out, ln1_g, ln1_b, p_wq, p_keys,
           p_u, p_v, ln2_g, ln2_b):
    bsz, t, d = x.shape
    n = bsz * t
    row = lambda a: a.reshape(1, -1)

    assert DEPTH == 1 and w_in.shape[0] == 1, "the entry LayerNorm is fused with the single layer's projection"
    l = 0
    wa = _layout_a(w_in[l][:, :A_IN]).astype(BF16)
    wb = _layout_b(w_in[l][:, A_IN:]).astype(BF16)
    h, pa, pb = ln_proj(x.reshape(n, d), row(ln_in_g), row(ln_in_b), wa, wb)
    ya = rwkv7(
        pa.reshape(bsz, t, A_PAD), _layout_a(row(a_mu[l])), row(a_w0[l]),
        _pad_rows(a_w2[l], LANES), row(a_a0[l]), _pad_rows(a_a2[l], LANES), a_g2[l],
        row(a_k_k[l]), row(a_k_a[l]), row(a_r_k[l]), row(a_gn_g[l]), row(a_gn_b[l]))
    yb = gla(pb.reshape(bsz, t, B_PAD), b_conv[l], _pad_rows(b_gk_w2[l], LANES),
             row(b_gk_b[l]), row(b_norm_g[l]))
    h1, h1b = out_ln(ya.reshape(n, A_WIDTH), yb.reshape(n, B_WIDTH), h,
                     w_out[l][:A_WIDTH].astype(BF16), w_out[l][A_WIDTH:].astype(BF16),
                     row(ln1_g[l]), row(ln1_b[l]))
    ws = fold_keys(p_keys[l].reshape(P_HEADS * 2, P_KEYS, P_HALF), p_wq[l])
    out = peer_fused(h1, h1b, ws, p_u[l].astype(BF16).T, p_v[l].astype(BF16),
                     row(ln2_g[l]), row(ln2_b[l]))
    return out.reshape(bsz, t, d)
```

```python
import functools
import math

import jax
import jax.numpy as jnp
from jax import lax
from jax.experimental import pallas as pl
from jax.experimental.pallas import tpu as pltpu

F32 = jnp.float32
BF16 = jnp.bfloat16
I32 = jnp.int32

D_MODEL = 1024
DEPTH = 1
CHUNK = 64
LN_EPS = 1e-5
ALPHA = (2.0 * DEPTH) ** 0.25

A_HEAD = 64
A_WIDTH = 512
A_HEADS = 8
A_GN_EPS = 64e-5
A_IN = 1792
A_PAD = 1920

B_HEADS = 4
B_DV = 128
B_DK = 64
B_WIDTH = 512
B_WIDTH_K = 256
B_QKV = 1024
B_GATE_LORA = 16
B_GATE_NORM = 16.0
B_CONV = 4
B_RMS_EPS = 1e-5
B_PAD = 1664

P_HEADS = 8
P_KEYS = 128
P_HALF = 128
P_TOPK = 16

LANES = 128
G_PITCH = 72

VMEM_LIMIT = 56 * 1024 * 1024


def _dot(a, b):
    return jnp.dot(a.astype(BF16), b.astype(BF16), preferred_element_type=F32)


def _dot_nt(a, b):
    return lax.dot_general(a.astype(BF16), b.astype(BF16), (((1,), (1,)), ((), ())),
                           preferred_element_type=F32)


def _dot_tn(a, b):
    return lax.dot_general(a.astype(BF16), b.astype(BF16), (((0,), (0,)), ((), ())),
                           preferred_element_type=F32)


def _split3(x):
    hi = x.astype(BF16)
    r1 = x - hi.astype(F32)
    mid = r1.astype(BF16)
    lo = (r1 - mid.astype(F32)).astype(BF16)
    return hi, mid, lo


def _dot_exact_rhs(a01, x):
    a = a01.astype(BF16)
    hi, mid, lo = _split3(x)
    out = jnp.dot(a, hi, preferred_element_type=F32)
    out += jnp.dot(a, mid, preferred_element_type=F32)
    out += jnp.dot(a, lo, preferred_element_type=F32)
    return out


def _dot_exact_lhs(x, b01):
    b = b01.astype(BF16)
    hi, mid, lo = _split3(x)
    out = jnp.dot(hi, b, preferred_element_type=F32)
    out += jnp.dot(mid, b, preferred_element_type=F32)
    out += jnp.dot(lo, b, preferred_element_type=F32)
    return out


def _sigmoid(x):
    return 1.0 / (1.0 + jnp.exp(-x))


def _softplus(x):
    return jnp.maximum(x, 0.0) + jnp.log(1.0 + jnp.exp(-jnp.abs(x)))


def _silu(x):
    return x * _sigmoid(x)


def _layer_norm(x, g, b):
    mu = jnp.mean(x, axis=-1, keepdims=True)
    xc = x - mu
    var = jnp.mean(xc * xc, axis=-1, keepdims=True)
    return xc * lax.rsqrt(var + LN_EPS) * g + b


def _block_diag_ones(n, seg):
    r = lax.broadcasted_iota(I32, (n, n), 0) // seg
    c = lax.broadcasted_iota(I32, (n, n), 1) // seg
    return jnp.where(r == c, 1.0, 0.0).astype(F32)


def _chunk_tril(n, seg):
    r = lax.broadcasted_iota(I32, (n, n), 0)
    c = lax.broadcasted_iota(I32, (n, n), 1)
    return jnp.where((r // seg == c // seg) & (c <= r), 1.0, 0.0).astype(F32)


def _ln_proj_kernel(x_ref, g_ref, b_ref, wa_ref, wb_ref, h_ref, pa_ref, pb_ref):
    h = _layer_norm(x_ref[...], g_ref[...], b_ref[...])
    h_ref[...] = h
    hb = h.astype(BF16)
    pa_ref[...] = jnp.dot(hb, wa_ref[...], preferred_element_type=F32)
    pb_ref[...] = jnp.dot(hb, wb_ref[...], preferred_element_type=F32)


def ln_proj(x2, g, b, wa, wb, tm=512):
    n = x2.shape[0]
    full = lambda i: (0, 0)
    return pl.pallas_call(
        _ln_proj_kernel,
        grid=(n // tm,),
        in_specs=[
            pl.BlockSpec((tm, D_MODEL), lambda i: (i, 0)),
            pl.BlockSpec((1, D_MODEL), full),
            pl.BlockSpec((1, D_MODEL), full),
            pl.BlockSpec((D_MODEL, A_PAD), full),
            pl.BlockSpec((D_MODEL, B_PAD), full),
        ],
        out_specs=[
            pl.BlockSpec((tm, D_MODEL), lambda i: (i, 0)),
            pl.BlockSpec((tm, A_PAD), lambda i: (i, 0)),
            pl.BlockSpec((tm, B_PAD), lambda i: (i, 0)),
        ],
        out_shape=[
            jax.ShapeDtypeStruct((n, D_MODEL), F32),
            jax.ShapeDtypeStruct((n, A_PAD), F32),
            jax.ShapeDtypeStruct((n, B_PAD), F32),
        ],
        compiler_params=pltpu.CompilerParams(
            dimension_semantics=("arbitrary",), vmem_limit_bytes=VMEM_LIMIT),
        name="ln_proj",
    )(x2, g, b, wa, wb)


def _rwkv_kernel(pa_ref, mu_ref, w0_ref, w2_ref, a0_ref, a2_ref, g2_ref, kkw_ref, kaw_ref,
                 rk_ref, gng_ref, gnb_ref, ya_ref,
                 carry_ref, z_ref, at_s, rt_s, bt_s, kt_s, bh_s, kh_s, v_s, el_s, y_s, *, tb):
    nck = tb // CHUNK
    w = A_WIDTH

    @pl.when(pl.program_id(1) == 0)
    def _():
        carry_ref[...] = jnp.zeros_like(carry_ref)
        z_ref[...] = jnp.zeros_like(z_ref)

    p = pa_ref[0]
    rolled = pltpu.roll(p, 1, axis=0)
    row = lax.broadcasted_iota(I32, p.shape, 0)
    prev = jnp.where(row == 0, carry_ref[0:1, :], rolled)
    carry_ref[0:1, :] = p[tb - 1:tb, :]
    ps = p + (prev - p) * mu_ref[...]

    r = ps[:, 0:w]
    k = ps[:, w:2 * w]
    v = ps[:, 2 * w:3 * w]
    wd = ps[:, 3 * w:3 * w + LANES]
    ad = ps[:, 3 * w + LANES:3 * w + 2 * LANES]
    gd = ps[:, 3 * w + 2 * LANES:3 * w + 3 * LANES]

    wlin = w0_ref[...] + _dot(jnp.tanh(wd), w2_ref[...])
    lw = -jnp.exp(-_softplus(-wlin) - 0.5)
    iclr = _sigmoid(a0_ref[...] + _dot(ad, a2_ref[...]))
    gate = _dot(_sigmoid(gd), g2_ref[...])

    bd = _block_diag_ones(w, A_HEAD)
    kkr = k * kkw_ref[...]
    ss = _dot_exact_lhs(kkr * kkr, bd)
    kk = kkr / jnp.maximum(jnp.sqrt(ss), 1e-12)
    kh = k * (1.0 + (iclr - 1.0) * kaw_ref[...])
    b = kk * iclr
    bonus = _dot_exact_lhs(r * kh * rk_ref[...], bd) * v

    lcum = _dot_exact_rhs(_chunk_tril(tb, CHUNK), lw)
    e_l = jnp.exp(lcum)
    e_nl = jnp.exp(-lcum)
    at_s[...] = -kk * jnp.exp(lcum - lw)
    rt_s[...] = r * e_l
    bt_s[...] = b * e_nl
    kt_s[...] = kh * e_nl
    v_s[...] = v
    for c in range(nck):
        rows = slice(c * CHUNK, (c + 1) * CHUNK)
        l_end = lcum[(c + 1) * CHUNK - 1:(c + 1) * CHUNK, :]
        e_lc = jnp.exp(l_end - lcum[rows, :])
        bh_s[rows, :] = b[rows, :] * e_lc
        kh_s[rows, :] = kh[rows, :] * e_lc
        el_s[c:c + 1, :] = jnp.exp(l_end)

    ri = lax.broadcasted_iota(I32, (2 * CHUNK, 2 * CHUNK), 0)
    ci = lax.broadcasted_iota(I32, (2 * CHUNK, 2 * CHUNK), 1) % CHUNK
    tri_mask = ci < jnp.where(ri < CHUNK, ri, ri - (CHUNK - 1))
    er = lax.broadcasted_iota(I32, (CHUNK, CHUNK), 0)
    ec = lax.broadcasted_iota(I32, (CHUNK, CHUNK), 1)
    eye_mask = er == ec
    eye = jnp.where(eye_mask, 1.0, 0.0).astype(F32)

    heads = range(A_HEADS)
    chains = [(c, h) for c in range(nck) for h in heads]
    rows = [slice(c * CHUNK, (c + 1) * CHUNK) for c in range(nck)]
    slabs = [{name: ref[rows[c], :] for name, ref in
              (("at", at_s), ("rt", rt_s), ("bt", bt_s), ("kt", kt_s), ("bh", bh_s), ("kh", kh_s), ("v", v_s))}
             for c in range(nck)]
    els = [el_s[c:c + 1, :] for c in range(nck)]

    def part(name, c, h):
        return slabs[c][name][:, h * A_HEAD:(h + 1) * A_HEAD]

    gms = [jnp.where(tri_mask,
                     _dot_nt(jnp.concatenate([part("at", c, h), part("rt", c, h)], axis=0),
                             jnp.concatenate([part("bt", c, h), part("kt", c, h)], axis=0)), 0.0)
           for c, h in chains]
    mvs = [_dot(gm[:, CHUNK:2 * CHUNK], part("v", c, h)) for gm, (c, h) in zip(gms, chains)]
    khv = [_dot_tn(part("kh", c, h), part("v", c, h)) for c, h in chains]
    pws = [gm[0:CHUNK, 0:CHUNK] for gm in gms]
    tms = [eye + pw for pw in pws]
    for _ in range(5):
        pws = [_dot(pw, pw) for pw in pws]
        tms = [tm + _dot(tm, pw) for tm, pw in zip(tms, pws)]
    p12s = [_dot(tm, jnp.concatenate([mv[0:CHUNK], part("at", c, h)], axis=1))
            for tm, mv, (c, h) in zip(tms, mvs, chains)]
    yqs = [jnp.concatenate([mv[CHUNK:2 * CHUNK], part("rt", c, h)], axis=1)
           + _dot(gm[CHUNK:2 * CHUNK, 0:CHUNK], p12)
           for mv, gm, p12, (c, h) in zip(mvs, gms, p12s, chains)]
    zws = [jnp.concatenate(
               [kv, jnp.where(eye_mask,
                              jnp.broadcast_to(els[c][:, h * A_HEAD:(h + 1) * A_HEAD], (CHUNK, CHUNK)), 0.0)],
               axis=1)
           + _dot_tn(part("bh", c, h), p12)
           for kv, p12, (c, h) in zip(khv, p12s, chains)]

    zs = [z_ref[h] for h in heads]
    for c in range(nck):
        ys = []
        for h in heads:
            yq, zw = yqs[c * A_HEADS + h], zws[c * A_HEADS + h]
            ys.append(yq[:, 0:CHUNK] + _dot(yq[:, CHUNK:2 * CHUNK], zs[h]))
            zs[h] = zw[:, 0:CHUNK] + _dot(zw[:, CHUNK:2 * CHUNK], zs[h])
        y_s[rows[c], :] = jnp.concatenate(ys, axis=1)
    for h in heads:
        z_ref[h] = zs[h]

    y = y_s[...]
    inv_n = 1.0 / A_HEAD
    mean = _dot_exact_lhs(y, bd) * inv_n
    yc = y - mean
    var = _dot_exact_lhs(yc * yc, bd) * inv_n
    yn = yc * lax.rsqrt(var + A_GN_EPS) * gng_ref[...] + gnb_ref[...]
    ya_ref[0] = (yn + bonus) * gate


def rwkv7(pa3, mu, w0, w2, a0, a2, g2, kkw, kaw, rk, gng, gnb, tb=256):
    bsz, t, _ = pa3.shape
    full = lambda bi, ti: (0, 0)
    vec = pl.BlockSpec((1, A_WIDTH), full)
    slab = pltpu.VMEM((tb, A_WIDTH), F32)
    return pl.pallas_call(
        functools.partial(_rwkv_kernel, tb=tb),
        grid=(bsz, t // tb),
        in_specs=[
            pl.BlockSpec((1, tb, A_PAD), lambda bi, ti: (bi, ti, 0)),
            pl.BlockSpec((1, A_PAD), full),
            vec,
            pl.BlockSpec((LANES, A_WIDTH), full),
            vec,
            pl.BlockSpec((LANES, A_WIDTH), full),
            pl.BlockSpec((LANES, A_WIDTH), full),
            vec, vec, vec, vec, vec,
        ],
        out_specs=pl.BlockSpec((1, tb, A_WIDTH), lambda bi, ti: (bi, ti, 0)),
        out_shape=jax.ShapeDtypeStruct((bsz, t, A_WIDTH), F32),
        scratch_shapes=[
            pltpu.VMEM((8, A_PAD), F32),
            pltpu.VMEM((A_HEADS, A_HEAD, A_HEAD), F32),
            slab, slab, slab, slab, slab, slab, slab,
            pltpu.VMEM((8, A_WIDTH), F32),
            slab,
        ],
        compiler_params=pltpu.CompilerParams(
            dimension_semantics=("arbitrary", "arbitrary"), vmem_limit_bytes=VMEM_LIMIT),
        name="rwkv7",
    )(pa3, mu, w0, w2, a0, a2, g2, kkw, kaw, rk, gng, gnb)


def _gla_kernel(pb_ref, cw_ref, gw_ref, gb_ref, ng_ref, yb_ref,
                xpad_ref, s_ref, qt_s, kt_s, kh_s, v_s, gk_s, o_s, *, tb):
    nck = tb // CHUNK

    @pl.when(pl.program_id(1) == 0)
    def _():
        xpad_ref[0:8, :] = jnp.zeros((8, B_QKV), F32)
        s_ref[...] = jnp.zeros_like(s_ref)

    p = pb_ref[0]
    xpad_ref[8:8 + tb, :] = p[:, 0:B_QKV]
    conv = cw_ref[B_CONV - 1:B_CONV, :] * p[:, 0:B_QKV]
    for j in range(B_CONV - 1):
        sh = B_CONV - 1 - j
        conv = conv + cw_ref[j:j + 1, :] * xpad_ref[8 - sh:8 - sh + tb, :]
    xpad_ref[0:8, :] = p[tb - 8:tb, 0:B_QKV]
    qkv = _silu(conv)
    q = qkv[:, 0:B_WIDTH_K] * (B_DK ** -0.5)
    k = qkv[:, B_WIDTH_K:2 * B_WIDTH_K]
    v_s[...] = qkv[:, 2 * B_WIDTH_K:]
    gkd = p[:, B_QKV:B_QKV + LANES]
    gz = _dot(gkd, gw_ref[...]) + gb_ref[...]
    gk = -_softplus(-gz) * (1.0 / B_GATE_NORM)
    gk_s[...] = gk
    gcum = _dot_exact_rhs(_chunk_tril(tb, CHUNK), gk)
    qt_s[...] = q * jnp.exp(gcum)
    kt_s[...] = k * jnp.exp(-gcum)
    for c in range(nck):
        rows = slice(c * CHUNK, (c + 1) * CHUNK)
        g_end = gcum[(c + 1) * CHUNK - 1:(c + 1) * CHUNK, :]
        kh_s[rows, :] = k[rows, :] * jnp.exp(g_end - gcum[rows, :])

    ri = lax.broadcasted_iota(I32, (CHUNK, CHUNK), 0)
    ci = lax.broadcasted_iota(I32, (CHUNK, CHUNK), 1)
    causal = ci <= ri
    ones_cv = jnp.ones((CHUNK, B_DV), F32)

    heads = range(B_HEADS)
    chains = [(c, h) for c in range(nck) for h in heads]
    rows = [slice(c * CHUNK, (c + 1) * CHUNK) for c in range(nck)]
    slabs = [{name: ref[rows[c], :] for name, ref in
              (("qt", qt_s), ("kt", kt_s), ("kh", kh_s), ("v", v_s), ("gk", gk_s))} for c in range(nck)]

    def key_part(name, c, h):
        return slabs[c][name][:, h * B_DK:(h + 1) * B_DK]

    def val_part(c, h):
        return slabs[c]["v"][:, h * B_DV:(h + 1) * B_DV]

    ob = ones_cv.astype(BF16)
    dn = (((0,), (0,)), ((), ()))
    intra = [_dot(jnp.where(causal, _dot_nt(key_part("qt", c, h), key_part("kt", c, h)), 0.0), val_part(c, h))
             for c, h in chains]
    kvs = [_dot_tn(key_part("kh", c, h), val_part(c, h)) for c, h in chains]
    decays = []
    for c, h in chains:
        hi, mid, lo = _split3(key_part("gk", c, h))
        gcol = (lax.dot_general(hi, ob, dn, preferred_element_type=F32)
                + lax.dot_general(mid, ob, dn, preferred_element_type=F32)
                + lax.dot_general(lo, ob, dn, preferred_element_type=F32))
        decays.append(jnp.exp(gcol))
    ss = [s_ref[h] for h in heads]
    for c in range(nck):
        os_ = []
        for h in heads:
            n = c * B_HEADS + h
            os_.append(intra[n] + _dot(key_part("qt", c, h), ss[h]))
            ss[h] = ss[h] * decays[n] + kvs[n]
        o_s[rows[c], :] = jnp.concatenate(os_, axis=1)
    for h in heads:
        s_ref[h] = ss[h]

    g = p[:, B_QKV + LANES:]
    outs = []
    for h in range(B_HEADS):
        sv = slice(h * B_DV, (h + 1) * B_DV)
        o = o_s[:, sv]
        o = o * lax.rsqrt(jnp.mean(o * o, axis=-1, keepdims=True) + B_RMS_EPS) * ng_ref[...]
        outs.append(o * _silu(g[:, sv]))
    yb_ref[0] = jnp.concatenate(outs, axis=1)


def gla(pb3, cw, gw, gb, ng, tb=256):
    bsz, t, _ = pb3.shape
    full = lambda bi, ti: (0, 0)
    return pl.pallas_call(
        functools.partial(_gla_kernel, tb=tb),
        grid=(bsz, t // tb),
        in_specs=[
            pl.BlockSpec((1, tb, B_PAD), lambda bi, ti: (bi, ti, 0)),
            pl.BlockSpec((B_CONV, B_QKV), full),
            pl.BlockSpec((LANES, B_WIDTH_K), full),
            pl.BlockSpec((1, B_WIDTH_K), full),
            pl.BlockSpec((1, B_DV), full),
        ],
        out_specs=pl.BlockSpec((1, tb, B_WIDTH), lambda bi, ti: (bi, ti, 0)),
        out_shape=jax.ShapeDtypeStruct((bsz, t, B_WIDTH), F32),
        scratch_shapes=[
            pltpu.VMEM((tb + 8, B_QKV), F32),
            pltpu.VMEM((B_HEADS, B_DK, B_DV), F32),
            pltpu.VMEM((tb, B_WIDTH_K), F32),
            pltpu.VMEM((tb, B_WIDTH_K), F32),
            pltpu.VMEM((tb, B_WIDTH_K), F32),
            pltpu.VMEM((tb, B_WIDTH), F32),
            pltpu.VMEM((tb, B_WIDTH_K), F32),
            pltpu.VMEM((tb, B_WIDTH), F32),
        ],
        compiler_params=pltpu.CompilerParams(
            dimension_semantics=("arbitrary", "arbitrary"), vmem_limit_bytes=VMEM_LIMIT),
        name="gla",
    )(pb3, cw, gw, gb, ng)


def _out_ln_kernel(ya_ref, yb_ref, h_ref, wa_ref, wb_ref, g_ref, b_ref, h1_ref, h1b_ref):
    y = _dot(ya_ref[...], wa_ref[...]) + _dot(yb_ref[...], wb_ref[...])
    h1 = _layer_norm(ALPHA * h_ref[...] + y, g_ref[...], b_ref[...])
    h1_ref[...] = h1
    h1b_ref[...] = h1.astype(BF16)


def out_ln(ya, yb, h, wa, wb, g, b, tm=512):
    n = h.shape[0]
    full = lambda i: (0, 0)
    return pl.pallas_call(
        _out_ln_kernel,
        grid=(n // tm,),
        in_specs=[
            pl.BlockSpec((tm, A_WIDTH), lambda i: (i, 0)),
            pl.BlockSpec((tm, B_WIDTH), lambda i: (i, 0)),
            pl.BlockSpec((tm, D_MODEL), lambda i: (i, 0)),
            pl.BlockSpec((A_WIDTH, D_MODEL), full),
            pl.BlockSpec((B_WIDTH, D_MODEL), full),
            pl.BlockSpec((1, D_MODEL), full),
            pl.BlockSpec((1, D_MODEL), full),
        ],
        out_specs=[
            pl.BlockSpec((tm, D_MODEL), lambda i: (i, 0)),
            pl.BlockSpec((tm, D_MODEL), lambda i: (i, 0)),
        ],
        out_shape=[
            jax.ShapeDtypeStruct((n, D_MODEL), F32),
            jax.ShapeDtypeStruct((n, D_MODEL), BF16),
        ],
        compiler_params=pltpu.CompilerParams(
            dimension_semantics=("arbitrary",), vmem_limit_bytes=VMEM_LIMIT),
        name="out_ln",
    )(ya, yb, h, wa, wb, g, b)


def _fold_keys_kernel(keys_ref, wq_ref, out_ref):
    kh, km, kl = _split3(keys_ref[0])
    wh, wm, wl = _split3(wq_ref[...])
    dn = (((1,), (1,)), ((), ()))
    acc = lax.dot_general(kh, wh, dn, preferred_element_type=F32)
    for a, b in ((kh, wm), (km, wh), (km, wm), (kh, wl), (kl, wh)):
        acc += lax.dot_general(a, b, dn, preferred_element_type=F32)
    out_ref[...] = acc.astype(BF16)


def fold_keys(keys3, wq):
    nhp = keys3.shape[0]
    return pl.pallas_call(
        _fold_keys_kernel,
        grid=(nhp,),
        in_specs=[
            pl.BlockSpec((1, P_KEYS, P_HALF), lambda i: (i, 0, 0)),
            pl.BlockSpec((D_MODEL, P_HALF), lambda i: (0, i)),
        ],
        out_specs=pl.BlockSpec((P_KEYS, D_MODEL), lambda i: (i, 0)),
        out_shape=jax.ShapeDtypeStruct((nhp * P_KEYS, D_MODEL), BF16),
        compiler_params=pltpu.CompilerParams(dimension_semantics=("arbitrary",)),
        name="fold_keys",
    )(keys3, wq)


def _cand_index(tt):
    r = lax.broadcasted_iota(I32, (72, tt), 0)
    c = jnp.where(r < 24, r, jnp.where(r < 32, r + 8, r + 16))
    c = jnp.where(r >= 40, (r - 40) * P_TOPK, c)
    c = jnp.where(r >= 56, (r - 56) * P_TOPK + 1, c)
    c = jnp.where(r >= 64, (r - 64) * P_TOPK + 2, c)
    pos = jnp.where(r >= 64, r - 64, jnp.where(r >= 56, r - 56, jnp.where(r >= 40, r - 40, 4)))
    return c.astype(F32), pos < 4


def _pick16(table, sel):
    out = jnp.zeros_like(table)
    for i in range(P_TOPK):
        out = jnp.where(sel == float(i), table[i:i + 1, :], out)
    return out


def _top16_steps(vals, idx):
    out_v, out_i = [], []
    for _ in range(P_TOPK):
        m = jnp.max(vals, axis=0, keepdims=True)
        i = jnp.min(jnp.where(vals == m, idx, float(2 ** 20)), axis=0, keepdims=True)
        out_v.append(m)
        out_i.append(i)
        vals = jnp.where(idx == i, -jnp.inf, vals)
        yield
    return jnp.concatenate(out_v, axis=0), jnp.concatenate(out_i, axis=0)


def _head_topk_steps(h, half, s_scr, e1s, e2s, gts, key_idx, cand_idx, cand_dup):
    r0 = pl.multiple_of(h * 2 * P_KEYS, 2 * P_KEYS)
    sv1, si1 = yield from _top16_steps(s_scr[half, pl.ds(r0, P_KEYS), :], key_idx)
    sv2, si2 = yield from _top16_steps(s_scr[half, pl.ds(r0 + P_KEYS, P_KEYS), :], key_idx)
    cand = jnp.concatenate(
        [sv1[0:1, :] + sv2]
        + [sv1[i:i + 1, :] + sv2[0:8, :] for i in (1, 2, 3)]
        + [sv1 + sv2[0:1, :]]
        + [sv1[0:8, :] + sv2[j:j + 1, :] for j in (1, 2)], axis=0)
    cv, cidx = yield from _top16_steps(jnp.where(cand_dup, -jnp.inf, cand), cand_idx)
    qi = jnp.floor(cidx * (1.0 / P_TOPK))
    ri = cidx - qi * P_TOPK
    ex = jnp.exp(cv - cv[0:1, :])
    o0 = pl.multiple_of(h * P_TOPK, P_TOPK)
    e1s[half, pl.ds(o0, P_TOPK), :] = _pick16(si1, qi)
    e2s[half, pl.ds(o0, P_TOPK), :] = _pick16(si2, ri)
    gts[half, pl.ds(o0, P_TOPK), :] = ex / jnp.sum(ex, axis=0, keepdims=True)


def _head_topk(*args, **kwargs):
    for _ in _head_topk_steps(*args, **kwargs):
        pass


def _gelu_x2(x):
    return x * (1.0 + lax.erf(x * (1.0 / math.sqrt(2.0))))


def _peer_fused_kernel(h1_ref, h1b_ref, h1bn_ref, ws_ref, ulo_ref, uhi_ref, vlo_ref, vhi_ref,
                       g_ref, b_ref, out_ref,
                       gs_ref, acc_ref, s_scr, e1s, e2s, gts, tm1, tm2, tmg, *, tt, ngrp):
    i = pl.program_id(0)
    j = pl.program_id(1)
    nsel = P_HEADS * P_TOPK
    half_keys = P_KEYS // 2
    th = tt // 2
    key_idx = lax.broadcasted_iota(I32, (P_KEYS, th), 0).astype(F32)
    cand_idx, cand_dup = _cand_index(th)
    topk = functools.partial(_head_topk, s_scr=s_scr, e1s=e1s, e2s=e2s, gts=gts,
                             key_idx=key_idx, cand_idx=cand_idx, cand_dup=cand_dup)

    def scores(hb_ref):
        for half in range(2):
            s_scr[half] = lax.dot_general(ws_ref[...], hb_ref[half * th:(half + 1) * th, :],
                                          (((1,), (1,)), ((), ())), preferred_element_type=F32)

    @pl.when((i == 0) & (j == 0))
    def _():
        scores(h1b_ref)

        def body(t, carry):
            topk(t // 2, t % 2)
            return carry

        lax.fori_loop(0, 2 * P_HEADS, body, 0)

    @pl.when(j == 0)
    def _():
        acc_ref[...] = jnp.zeros_like(acc_ref)
        for half in range(2):
            rows = slice(half * th, (half + 1) * th)
            tm1[rows, :] = e1s[half].T
            tm2[rows, :] = e2s[half].T
            tmg[rows, :] = gts[half].T
        sub = lax.broadcasted_iota(I32, (P_KEYS, nsel), 0).astype(F32)
        zero = jnp.zeros((P_KEYS, nsel), F32)

        def pair_body(m, carry):
            n0 = 2 * m
            onehots = []
            gated = []
            for q in range(2):
                e1 = tm1[pl.ds(n0 + q, 1), :]
                e2 = tm2[pl.ds(n0 + q, 1), :]
                gt = 0.5 * tmg[pl.ds(n0 + q, 1), :]
                gated.append(jnp.where(sub == e1, gt, 0.0))
                onehots.append(jnp.where(sub == e2, 1.0, 0.0))
            a_cat = jnp.concatenate(gated, axis=1).astype(BF16)
            b_bd = jnp.concatenate(
                [jnp.concatenate([onehots[0], zero], axis=1),
                 jnp.concatenate([zero, onehots[1]], axis=1)], axis=0).astype(BF16)
            gn = lax.dot_general(a_cat, b_bd, (((1,), (1,)), ((), ())),
                                 preferred_element_type=F32)
            bits = pltpu.bitcast(gn, I32) + 0x8000
            packed = (bits[half_keys:, :] & -65536) | lax.shift_right_logical(bits[:half_keys, :], 16)
            for q in range(2):
                r0 = pl.multiple_of((n0 + q) * G_PITCH, 8)
                gs_ref[pl.ds(r0, half_keys), :] = packed[:, q * P_KEYS:(q + 1) * P_KEYS]
            return carry

        lax.fori_loop(0, tt // 2, pair_body, 0, unroll=64)
        scores(h1bn_ref)

    nper = 2 * P_HEADS // (half_keys // ngrp)
    units = [_head_topk_steps((j * nper + u) // 2, (j * nper + u) % 2, s_scr, e1s, e2s, gts,
                              key_idx, cand_idx, cand_dup) for u in range(nper)]
    msplit = 2
    npiece = msplit * (2 * (ngrp // 2) + 2)
    per_piece = -(-3 * P_TOPK * nper // npiece)
    tm = tt // msplit

    def advance(count):
        for _ in range(count):
            while units:
                try:
                    next(units[0])
                    break
                except StopIteration:
                    units.pop(0)

    sub = 2 * P_KEYS
    for mi in range(msplit):
        trow = slice(mi * tm, (mi + 1) * tm)
        x = h1b_ref[trow, :]
        h_lo, h_hi = [], []
        for k in range(ngrp // 2):
            words = [gs_ref[pl.ds((mi * tm) * G_PITCH + j * ngrp + 2 * k + q, tm, stride=G_PITCH), :]
                     for q in range(2)]
            cols = slice(k * sub, (k + 1) * sub)
            g_lo = jnp.concatenate([pltpu.bitcast(w << 16, F32) for w in words], axis=1)
            s_lo = jnp.dot(x, ulo_ref[:, cols], preferred_element_type=F32)
            h_lo.append((g_lo * _gelu_x2(s_lo)).astype(BF16))
            advance(per_piece)
            g_hi = jnp.concatenate([pltpu.bitcast(w & -65536, F32) for w in words], axis=1)
            s_hi = jnp.dot(x, uhi_ref[:, cols], preferred_element_type=F32)
            h_hi.append((g_hi * _gelu_x2(s_hi)).astype(BF16))
            advance(per_piece)
        acc = acc_ref[trow, :]
        acc = acc + jnp.dot(jnp.concatenate(h_lo, axis=1), vlo_ref[...], preferred_element_type=F32)
        advance(per_piece)
        acc = acc + jnp.dot(jnp.concatenate(h_hi, axis=1), vhi_ref[...], preferred_element_type=F32)
        advance(per_piece)
        acc_ref[trow, :] = acc
    advance(4 * P_TOPK * nper)

    @pl.when(j == pl.num_programs(1) - 1)
    def _():
        out_ref[...] = _layer_norm(ALPHA * h1_ref[...] + acc_ref[...], g_ref[...], b_ref[...])


def peer_fused(h1, h1b, ws, ubt, vb, g, b, tt=512, ngrp=4):
    n = h1.shape[0]
    nsel = P_HEADS * P_TOPK
    nrow = P_HEADS * 2 * P_KEYS
    ntile = n // tt
    nstep = P_KEYS // 2 // ngrp
    assert (2 * P_HEADS) % nstep == 0, "each step runs a whole number of (head, token half) top-k units"
    rows = ngrp * P_KEYS
    th = tt // 2
    tok = lambda i, j: (i, 0)
    full = lambda i, j: (0, 0)
    once = pl.Buffered(1)
    sel = pltpu.VMEM((2, nsel, th), F32)
    tmaj = pltpu.VMEM((tt, nsel), F32)
    return pl.pallas_call(
        functools.partial(_peer_fused_kernel, tt=tt, ngrp=ngrp),
        grid=(ntile, nstep),
        in_specs=[
            pl.BlockSpec((tt, D_MODEL), tok, pipeline_mode=once),
            pl.BlockSpec((tt, D_MODEL), tok, pipeline_mode=once),
            pl.BlockSpec((tt, D_MODEL), lambda i, j: (jnp.minimum(i + 1, ntile - 1), 0), pipeline_mode=once),
            pl.BlockSpec((nrow, D_MODEL), full, pipeline_mode=once),
            pl.BlockSpec((D_MODEL, rows), lambda i, j: (0, j)),
            pl.BlockSpec((D_MODEL, rows), lambda i, j: (0, j + nstep)),
            pl.BlockSpec((rows, D_MODEL), lambda i, j: (j, 0)),
            pl.BlockSpec((rows, D_MODEL), lambda i, j: (j + nstep, 0)),
            pl.BlockSpec((1, D_MODEL), full),
            pl.BlockSpec((1, D_MODEL), full),
        ],
        out_specs=pl.BlockSpec((tt, D_MODEL), tok),
        out_shape=jax.ShapeDtypeStruct((n, D_MODEL), F32),
        scratch_shapes=[
            pltpu.VMEM((tt * G_PITCH, P_KEYS), I32),
            pltpu.VMEM((tt, D_MODEL), F32),
            pltpu.VMEM((2, nrow, th), F32),
            sel, sel, sel,
            tmaj, tmaj, tmaj,
        ],
        compiler_params=pltpu.CompilerParams(
            dimension_semantics=("arbitrary", "arbitrary"), vmem_limit_bytes=VMEM_LIMIT),
        name="peer_fused",
    )(h1, h1b, h1b, ws, ubt, ubt, vb, vb, g, b)


def _pad_cols(w, width):
    return jnp.pad(w, ((0, 0), (0, width - w.shape[1])))


def _pad_rows(w, height):
    return jnp.pad(w, ((0, height - w.shape[0]), (0, 0)))


def _layout_a(w):
    w3 = 3 * A_WIDTH
    return jnp.concatenate([
        w[:, 0:w3], _pad_cols(w[:, w3:w3 + 64], LANES), _pad_cols(w[:, w3 + 64:w3 + 128], LANES),
        w[:, w3 + 128:A_IN]], axis=1)


def _layout_b(w):
    return jnp.concatenate([
        w[:, 0:B_QKV], _pad_cols(w[:, B_QKV:B_QKV + B_GATE_LORA], LANES),
        w[:, B_QKV + B_GATE_LORA:]], axis=1)


def kernel(x, ln_in_g, ln_in_b, w_in, a_mu, a_w0, a_w2, a_a0, a_a2, a_g2, a_k_k, a_k_a, a_r_k,
           a_gn_g, a_gn_b, b_conv, b_gk_w2, b_gk_b, b_norm_g, w_out, ln1_g, ln1_b, p_wq, p_keys,
           p_u, p_v, ln2_g, ln2_b):
    bsz, t, d = x.shape
    n = bsz * t
    row = lambda a: a.reshape(1, -1)

    assert DEPTH == 1 and w_in.shape[0] == 1, "the entry LayerNorm is fused with the single layer's projection"
    l = 0
    wa = _layout_a(w_in[l][:, :A_IN]).astype(BF16)
    wb = _layout_b(w_in[l][:, A_IN:]).astype(BF16)
    h, pa, pb = ln_proj(x.reshape(n, d), row(ln_in_g), row(ln_in_b), wa, wb)
    ya = rwkv7(
        pa.reshape(bsz, t, A_PAD), _layout_a(row(a_mu[l])), row(a_w0[l]),
        _pad_rows(a_w2[l], LANES), row(a_a0[l]), _pad_rows(a_a2[l], LANES), a_g2[l],
        row(a_k_k[l]), row(a_k_a[l]), row(a_r_k[l]), row(a_gn_g[l]), row(a_gn_b[l]))
    yb = gla(pb.reshape(bsz, t, B_PAD), b_conv[l], _pad_rows(b_gk_w2[l], LANES),
             row(b_gk_b[l]), row(b_norm_g[l]))
    h1, h1b = out_ln(ya.reshape(n, A_WIDTH), yb.reshape(n, B_WIDTH), h,
                     w_out[l][:A_WIDTH].astype(BF16), w_out[l][A_WIDTH:].astype(BF16),
                     row(ln1_g[l]), row(ln1_b[l]))
    ws = fold_keys(p_keys[l].reshape(P_HEADS * 2, P_KEYS, P_HALF), p_wq[l])
    out = peer_fused(h1, h1b, ws, p_u[l].astype(BF16).T, p_v[l].astype(BF16),
                     row(ln2_g[l]), row(ln2_b[l]))
    return out.reshape(bsz, t, d)
```
